```python
import jax, jax.numpy as jnp
from jax import lax
import numpy as np

D_MODEL = 2048
BATCH = 4
SEQ = 2048
DEPTH = 4
DEC_BATCH = 16
DEC_SEQ = 64
PAST_LEN = 1024

CHUNK = 64
N_MIXERS = 4
D_FF = 5632
N_MOD = 9
MACARON = 0.5
EPS = 1e-6
CONV_WIDTH = 31
CONV_STATE = CONV_WIDTH - 1
POOL_WINDOWS = (2, 4, 8, 16)
POOL_GROUPS = len(POOL_WINDOWS)
POOL_GW = D_MODEL // POOL_GROUPS
POOL_STATE = max(POOL_WINDOWS) - 1
HEAD_DIM = 64
N_HEADS = D_MODEL // HEAD_DIM
N_KV_HEADS = N_HEADS // 8
Q_PER_KV = N_HEADS // N_KV_HEADS
WINDOW = 128
SWA_BAND = -(-WINDOW // CHUNK) * CHUNK
ATTN_SCALE = HEAD_DIM ** -0.5
GMLP_CHUNK = 128
GMLP_WIDTH = 2 * D_MODEL
GMLP_GROUPS = 8
GMLP_GW = GMLP_WIDTH // GMLP_GROUPS

kernel_name = 'hybrid_streaming_encoder_step'


def _rmsnorm(x, g):
    xf = x.astype(jnp.float32)
    y = xf * lax.rsqrt(jnp.mean(xf * xf, axis=-1, keepdims=True) + EPS)
    return (y * g.astype(jnp.float32)).astype(x.dtype)


def _layernorm(x, g, b):
    xf = x.astype(jnp.float32)
    xc = xf - jnp.mean(xf, axis=-1, keepdims=True)
    var = jnp.mean(xc * xc, axis=-1, keepdims=True)
    y = xc * lax.rsqrt(var + EPS) * g.astype(jnp.float32) + b.astype(jnp.float32)
    return y.astype(x.dtype)


def _modulate(h, shift, scale):
    return h * (1 + scale[:, None, :]) + shift[:, None, :]


def _swiglu(h, w1, w3, w2):
    return (jax.nn.silu(h @ w1) * (h @ w3)) @ w2


def _conv_module(h, buf, w_pw1, b_pw1, w_dw, b_dw, ln_g, ln_b, w_pw2, b_pw2):
    a = h @ w_pw1 + b_pw1
    g = a[..., :D_MODEL] * jax.nn.sigmoid(a[..., D_MODEL:])
    ext = jnp.concatenate([buf.astype(g.dtype), g], axis=1)
    y = lax.conv_general_dilated(ext, w_dw[:, None, :].astype(ext.dtype), (1,), 'VALID',
                                 dimension_numbers=('NWC', 'WIO', 'NWC'),
                                 feature_group_count=D_MODEL) + b_dw
    y = jax.nn.silu(_layernorm(y, ln_g, ln_b))
    return y @ w_pw2 + b_pw2, ext[:, -CONV_STATE:]


def _pool_mixer(h, buf, pos0, w_in, w_grp, scale, w_out):
    B, T, _ = h.shape
    p = h @ w_in
    ext = jnp.concatenate([buf.astype(p.dtype), p], axis=1)
    cs = jnp.pad(jnp.cumsum(ext.astype(jnp.float32), axis=1), ((0, 0), (1, 0), (0, 0)))
    hi = cs[:, POOL_STATE + 1:]
    pos = pos0 + jnp.arange(T)
    pf = p.astype(jnp.float32)
    groups = []
    for gi, w in enumerate(POOL_WINDOWS):
        c0, c1 = gi * POOL_GW, (gi + 1) * POOL_GW
        lo = cs[:, POOL_STATE + 1 - w: POOL_STATE + 1 - w + T, c0:c1]
        cnt = jnp.minimum(pos + 1, w).astype(jnp.float32)[None, :, None]
        groups.append((hi[..., c0:c1] - lo) / cnt - pf[..., c0:c1])
    pooled = jnp.stack(groups, axis=2).astype(p.dtype)
    z = jnp.einsum('btgc,gcd->btgd', pooled, w_grp).reshape(B, T, D_MODEL) * scale
    return z @ w_out, ext[:, -POOL_STATE:]


def _alibi_slopes():
    return jnp.exp2(-8.0 * jnp.arange(1, N_HEADS + 1, dtype=jnp.float32) / N_HEADS)


def _attend(qb, kb, vb, dist, valid, sinks):
    slopes = _alibi_slopes().reshape(N_KV_HEADS, Q_PER_KV)
    s = jnp.einsum('bnqkgd,bnskd->bnkgqs', qb, kb, preferred_element_type=jnp.float32) * ATTN_SCALE
    s = s - slopes[None, None, :, :, None, None] * dist[None, :, None, None, :, :]
    s = jnp.where(valid[None, :, None, None, None, :], s, -jnp.inf)
    sk = sinks.astype(jnp.float32).reshape(N_KV_HEADS, Q_PER_KV)[None, None, :, :, None, None]
    m = jnp.maximum(jnp.max(s, axis=-1, keepdims=True), sk)
    e = jnp.exp(s - m)
    p = e / (jnp.sum(e, axis=-1, keepdims=True) + jnp.exp(sk - m))
    return jnp.einsum('bnkgqs,bnskd->bnqkgd', p.astype(vb.dtype), vb)


def _swa_prompt(h, wq, wk, wv, wo, sinks):
    B, T, _ = h.shape
    nb = T // CHUNK
    q = (h @ wq).reshape(B, nb, CHUNK, N_KV_HEADS, Q_PER_KV, HEAD_DIM)
    k = (h @ wk).reshape(B, T, N_KV_HEADS, HEAD_DIM)
    v = (h @ wv).reshape(B, T, N_KV_HEADS, HEAD_DIM)
    padw = ((0, 0), (SWA_BAND, 0), (0, 0), (0, 0))
    kp, vp = jnp.pad(k, padw), jnp.pad(v, padw)
    idx = jnp.arange(nb)[:, None] * CHUNK + jnp.arange(SWA_BAND + CHUNK)[None, :]
    kpos = idx - SWA_BAND
    qpos = jnp.arange(T).reshape(nb, CHUNK)
    dist = jnp.abs(qpos[:, :, None] - kpos[:, None, :]).astype(jnp.float32)
    o = _attend(q, kp[:, idx], vp[:, idx], dist, kpos >= 0, sinks)
    r = min(SWA_BAND, PAST_LEN)
    return o.reshape(B, T, N_HEADS * HEAD_DIM) @ wo, k[:, T - r:], v[:, T - r:]


def _swa_sample(h, k_buf, v_buf, wq, wk, wv, wo, sinks):
    B, T, _ = h.shape
    q = (h @ wq).reshape(B, 1, T, N_KV_HEADS, Q_PER_KV, HEAD_DIM)
    k = (h @ wk).reshape(B, T, N_KV_HEADS, HEAD_DIM)
    v = (h @ wv).reshape(B, T, N_KV_HEADS, HEAD_DIM)
    kk = jnp.concatenate([k_buf.astype(k.dtype), k], axis=1)
    vv = jnp.concatenate([v_buf.astype(v.dtype), v], axis=1)
    r = k_buf.shape[1]
    kpos = PAST_LEN - r + jnp.arange(r + T)
    qpos = PAST_LEN + jnp.arange(T)
    dist = jnp.abs(qpos[:, None] - kpos[None, :]).astype(jnp.float32)[None]
    valid = jnp.ones((1, r + T), dtype=bool)
    o = _attend(q, kk[:, None], vv[:, None], dist, valid, sinks)
    return o.reshape(B, T, N_HEADS * HEAD_DIM) @ wo, kk[:, T:], vv[:, T:]


def _gmlp(h, w_in, b_in, ln_g, ln_b, w_s, b_s, w_out, b_out, block):
    B, T, _ = h.shape
    z = jax.nn.gelu(h @ w_in + b_in)
    u, v = z[..., :GMLP_WIDTH], z[..., GMLP_WIDTH:]
    vn = _layernorm(v, ln_g, ln_b)
    L = block
    nc = T // L
    pos = jnp.arange(L)
    mask = (pos[None, :] // CHUNK) <= (pos[:, None] // CHUNK)
    ws = jnp.where(mask[None], w_s[:, :L, :L], 0)
    mixed = jnp.einsum('gts,bnsgc->bntgc', ws, vn.reshape(B, nc, L, GMLP_GROUPS, GMLP_GW))
    mixed = mixed + b_s[:, :L].T[None, None, :, :, None]
    out = (u * mixed.reshape(B, T, GMLP_WIDTH)) @ w_out + b_out
    return out, vn


def setup_inputs(seed: int = 0) -> dict:
    key = jax.random.key(seed)
    keys = iter(jax.random.split(key, 48))

    def nrm(shape, scale=1.0):
        return jax.random.normal(next(keys), shape, jnp.float32) * scale

    def gain(shape):
        return 1.0 + nrm(shape, 0.01)

    d, f = D_MODEL, D_FF
    r = min(SWA_BAND, PAST_LEN)
    hw = N_HEADS * HEAD_DIM
    kvw = N_KV_HEADS * HEAD_DIM
    return {
        'x_prompt': nrm((BATCH, SEQ, d)),
        'x_sample': nrm((DEC_BATCH, DEC_SEQ, d)),
        'c_prompt': nrm((BATCH, d)),
        'c_sample': nrm((DEC_BATCH, d)),
        'cache_conv': nrm((DEC_BATCH, CONV_STATE, d), 0.5),
        'cache_pool': nrm((DEC_BATCH, POOL_STATE, d)),
        'cache_swa_k': nrm((DEC_BATCH, r, N_KV_HEADS, HEAD_DIM)),
        'cache_swa_v': nrm((DEC_BATCH, r, N_KV_HEADS, HEAD_DIM)),
        'ada_w': nrm((DEPTH, d, N_MOD * d), 0.5 * d ** -0.5),
        'ada_b': nrm((DEPTH, N_MOD * d), 0.01),
        'norm_g': gain((DEPTH, 3, d)),
        'ffn_w1': nrm((DEPTH, 2, d, f), d ** -0.5),
        'ffn_w3': nrm((DEPTH, 2, d, f), d ** -0.5),
        'ffn_w2': nrm((DEPTH, 2, f, d), f ** -0.5),
        'conv_w_pw1': nrm((d, 2 * d), d ** -0.5),
        'conv_b_pw1': nrm((2 * d,), 0.01),
        'conv_w_dw': nrm((CONV_WIDTH, d), CONV_WIDTH ** -0.5),
        'conv_b_dw': nrm((d,), 0.01),
        'conv_ln_g': gain((d,)),
        'conv_ln_b': nrm((d,), 0.01),
        'conv_w_pw2': nrm((d, d), d ** -0.5),
        'conv_b_pw2': nrm((d,), 0.01),
        'pool_w_in': nrm((d, d), d ** -0.5),
        'pool_w_grp': nrm((POOL_GROUPS, POOL_GW, POOL_GW), POOL_GW ** -0.5),
        'pool_scale': 1.0 + nrm((d,), 0.1),
        'pool_w_out': nrm((d, d), d ** -0.5),
        'swa_wq': nrm((d, hw), d ** -0.5),
        'swa_wk': nrm((d, kvw), d ** -0.5),
        'swa_wv': nrm((d, kvw), d ** -0.5),
        'swa_wo': nrm((hw, d), hw ** -0.5),
        'swa_sinks': nrm((N_HEADS,), 0.5),
        'gmlp_w_in': nrm((d, 2 * GMLP_WIDTH), d ** -0.5),
        'gmlp_b_in': nrm((2 * GMLP_WIDTH,), 0.01),
        'gmlp_ln_g': gain((GMLP_WIDTH,)),
        'gmlp_ln_b': nrm((GMLP_WIDTH,), 0.01),
        'gmlp_w_s': nrm((GMLP_GROUPS, GMLP_CHUNK, GMLP_CHUNK), GMLP_CHUNK ** -0.5),
        'gmlp_b_s': 1.0 + nrm((GMLP_GROUPS, GMLP_CHUNK), 0.01),
        'gmlp_w_out': nrm((GMLP_WIDTH, d), GMLP_WIDTH ** -0.5),
        'gmlp_b_out': nrm((d,), 0.01),
        'final_g': gain((d,)),
    }


def reference(x_prompt, x_sample, c_prompt, c_sample,
              cache_conv, cache_pool, cache_swa_k, cache_swa_v,
              ada_w, ada_b, norm_g, ffn_w1, ffn_w3, ffn_w2,
              conv_w_pw1, conv_b_pw1, conv_w_dw, conv_b_dw, conv_ln_g, conv_ln_b, conv_w_pw2, conv_b_pw2,
              pool_w_in, pool_w_grp, pool_scale, pool_w_out,
              swa_wq, swa_wk, swa_wv, swa_wo, swa_sinks,
              gmlp_w_in, gmlp_b_in, gmlp_ln_g, gmlp_ln_b, gmlp_w_s, gmlp_b_s, gmlp_w_out, gmlp_b_out,
              final_g):

    def trunk(x, c, conv_buf, pool_buf, k_buf, v_buf, is_sample):
        pos0 = PAST_LEN if is_sample else 0
        cond = jax.nn.silu(c)
        conv_new = pool_new = k_new = v_new = gmlp_new = None
        for i in range(DEPTH):
            mod = cond @ ada_w[i] + ada_b[i]
            sh1, sc1, g1, sh2, sc2, g2, sh3, sc3, g3 = jnp.split(mod, N_MOD, axis=-1)
            h = _modulate(_rmsnorm(x, norm_g[i, 0]), sh1, sc1)
            x = x + MACARON * g1[:, None, :] * _swiglu(h, ffn_w1[i, 0], ffn_w3[i, 0], ffn_w2[i, 0])
            h = _modulate(_rmsnorm(x, norm_g[i, 1]), sh2, sc2)
            kind = i % N_MIXERS
            if kind == 0:
                out, conv_new = _conv_module(h, conv_buf, conv_w_pw1, conv_b_pw1, conv_w_dw, conv_b_dw,
                                             conv_ln_g, conv_ln_b, conv_w_pw2, conv_b_pw2)
            elif kind == 1:
                out, pool_new = _pool_mixer(h, pool_buf, pos0, pool_w_in, pool_w_grp, pool_scale, pool_w_out)
            elif kind == 2:
                if is_sample:
                    out, k_new, v_new = _swa_sample(h, k_buf, v_buf, swa_wq, swa_wk, swa_wv, swa_wo, swa_sinks)
                else:
                    out, k_new, v_new = _swa_prompt(h, swa_wq, swa_wk, swa_wv, swa_wo, swa_sinks)
            else:
                block = h.shape[1] if is_sample else GMLP_CHUNK
                out, gmlp_new = _gmlp(h, gmlp_w_in, gmlp_b_in, gmlp_ln_g, gmlp_ln_b, gmlp_w_s, gmlp_b_s,
                                      gmlp_w_out, gmlp_b_out, block)
            x = x + g2[:, None, :] * out
            h = _modulate(_rmsnorm(x, norm_g[i, 2]), sh3, sc3)
            x = x + MACARON * g3[:, None, :] * _swiglu(h, ffn_w1[i, 1], ffn_w3[i, 1], ffn_w2[i, 1])
        return _rmsnorm(x, final_g), conv_new, pool_new, k_new, v_new, gmlp_new

    bp = x_prompt.shape[0]
    zero_conv = jnp.zeros((bp, CONV_STATE, D_MODEL), x_prompt.dtype)
    zero_pool = jnp.zeros((bp, POOL_STATE, D_MODEL), x_prompt.dtype)
    y_prompt, conv_p, pool_p, k_p, v_p, _ = trunk(x_prompt, c_prompt, zero_conv, zero_pool, None, None, False)
    y_sample, conv_s, pool_s, k_s, v_s, gmlp_v_s = trunk(x_sample, c_sample, cache_conv, cache_pool,
                                                         cache_swa_k, cache_swa_v, True)
    return (y_prompt, y_sample, conv_p, conv_s, pool_p, pool_s, k_p, v_p, k_s, v_s, gmlp_v_s)
```

```python
import functools

import jax
import jax.numpy as jnp
from jax import lax
from jax.experimental import pallas as pl
from jax.experimental.pallas import tpu as pltpu

F32 = jnp.float32
BF16 = jnp.bfloat16

EPS = 1e-6
MACARON = 0.5
N_MOD = 9
CONV_WIDTH = 31
CONV_STATE = CONV_WIDTH - 1
POOL_WINDOWS = (2, 4, 8, 16)
POOL_STATE = max(POOL_WINDOWS) - 1
HEAD_DIM = 64
N_KV_HEADS = 4
Q_PER_KV = 8
N_HEADS = N_KV_HEADS * Q_PER_KV
CHUNK = 64
SWA_BAND = 128
ATTN_SCALE = HEAD_DIM ** -0.5
GMLP_CHUNK = 128
GMLP_GROUPS = 8
PAST_LEN = 1024

ROW_CHUNK = 16
CONV_HALO = 32
POOL_HALO = 16
SEQ_PAD = 32

V7X_VMEM_LIMIT_CAP = 60 * 1024 * 1024


class Layout:
    def __init__(self, bp, sp, bs, ts):
        self.bp, self.sp, self.bs, self.ts = bp, sp, bs, ts
        self.tp = bp * sp
        self.tsamp = bs * ts
        self.t = self.tp + self.tsamp
        assert sp % GMLP_CHUNK == 0 and ts == CHUNK and self.tsamp % GMLP_CHUNK == 0
        assert bp + bs <= SEQ_PAD

    def seq_of(self, row0):
        return jnp.where(row0 < self.tp, row0 // self.sp, self.bp + (row0 - self.tp) // self.ts)


def _pick(total, candidates):
    for c in candidates:
        if total % c == 0:
            return c
    raise ValueError(f"no tile for {total} in {candidates}")


def _vmem_limit(pipelined_bytes, scratch_bytes=0, temp_bytes=0):
    est = 2 * pipelined_bytes + scratch_bytes + temp_bytes + (4 << 20)
    return int(min(max(est, 16 << 20), V7X_VMEM_LIMIT_CAP))


def _nbytes(shape, dtype):
    n = 1
    for s in shape:
        n *= s
    return n * jnp.dtype(dtype).itemsize


def _dot(a, b):
    return jnp.dot(a, b, preferred_element_type=F32)


def _norm_mod_rows(h_ref, x_ref, mod_ref, ng_ref, row_base, tm, lay):
    ng = ng_ref[...]

    def body(c, carry):
        r = pl.multiple_of(c * ROW_CHUNK, ROW_CHUNK)
        m = mod_ref[lay.seq_of(row_base + r)]
        xg = x_ref[pl.ds(r, ROW_CHUNK), :]
        ms = jnp.mean(xg * xg, axis=-1, keepdims=True)
        y = xg * lax.rsqrt(ms + EPS) * ng
        h_ref[pl.ds(r, ROW_CHUNK), :] = (y * (1.0 + m[1:2, :]) + m[0:1, :]).astype(h_ref.dtype)
        return carry

    lax.fori_loop(0, tm // ROW_CHUNK, body, 0)


def _gated_residual_rows(o_ref, x_ref, mod_ref, b_ref, row_base, tm, lay, factor):
    def body(c, carry):
        r = pl.multiple_of(c * ROW_CHUNK, ROW_CHUNK)
        m = mod_ref[lay.seq_of(row_base + r)]
        acc = o_ref[pl.ds(r, ROW_CHUNK), :]
        if b_ref is not None:
            acc = acc + b_ref[...]
        o_ref[pl.ds(r, ROW_CHUNK), :] = x_ref[pl.ds(r, ROW_CHUNK), :] + (factor * m[2:3, :]) * acc
        return carry

    lax.fori_loop(0, tm // ROW_CHUNK, body, 0)


def _accumulate(o_ref, p, step):
    @pl.when(step == 0)
    def _():
        o_ref[...] = p

    @pl.when(step > 0)
    def _():
        o_ref[...] += p


def _ada_kernel(c_ref, w_ref, b_ref, o_ref):
    c = c_ref[...]
    cond = (c * jax.nn.sigmoid(c)).astype(BF16)
    o_ref[...] = _dot(cond, w_ref[...].astype(BF16)) + b_ref[...]


def _ada_table(c_all, ada_w, ada_b):
    depth, d, n = ada_w.shape
    tn = _pick(n, (1024, 512, 256, 128))
    blocks = _nbytes((SEQ_PAD, d), F32) + _nbytes((d, tn), F32) + _nbytes((1, tn), F32) + _nbytes((SEQ_PAD, tn), F32)
    return pl.pallas_call(
        _ada_kernel,
        grid=(depth, n // tn),
        in_specs=[
            pl.BlockSpec((SEQ_PAD, d), lambda l, j: (0, 0)),
            pl.BlockSpec((None, d, tn), lambda l, j: (l, 0, j)),
            pl.BlockSpec((None, 1, tn), lambda l, j: (l, 0, j)),
        ],
        out_specs=pl.BlockSpec((None, SEQ_PAD, tn), lambda l, j: (l, 0, j)),
        out_shape=jax.ShapeDtypeStruct((depth, SEQ_PAD, n), F32),
        compiler_params=pltpu.CompilerParams(
            dimension_semantics=("parallel", "arbitrary"),
            vmem_limit_bytes=_vmem_limit(blocks, temp_bytes=_nbytes((d, tn), BF16))),
        name="ada_table",
    )(c_all, ada_w, ada_b.reshape(depth, 1, n))


def _ffn_kernel(x_ref, mod_ref, ng_ref, w1_ref, w3_ref, w2_ref, o_ref, h_ref, *, tm, nj, lay):
    i = pl.program_id(0)
    j = pl.program_id(1)

    @pl.when(j == 0)
    def _():
        _norm_mod_rows(h_ref, x_ref, mod_ref, ng_ref, i * tm, tm, lay)

    h = h_ref[...]
    a = _dot(h, w1_ref[...])
    b = _dot(h, w3_ref[...])
    g = (a * jax.nn.sigmoid(a) * b).astype(BF16)
    _accumulate(o_ref, _dot(g, w2_ref[...]), j)

    @pl.when(j == nj - 1)
    def _():
        _gated_residual_rows(o_ref, x_ref, mod_ref, None, i * tm, tm, lay, MACARON)


def _ffn(x, mod, ng, w1, w3, w2, lay):
    t, d = x.shape
    f = w1.shape[1]
    tm = _pick(t, (768, 512, 384, 256, 128))
    tf = _pick(f, (512, 256, 128))
    nj = f // tf
    blocks = (2 * _nbytes((tm, d), F32) + _nbytes(mod.shape, F32) + 2 * _nbytes((d, tf), BF16)
              + _nbytes((tf, d), BF16))
    temps = 2 * _nbytes((tm, tf), F32) + _nbytes((tm, tf), BF16) + _nbytes((tm, d), F32)
    return pl.pallas_call(
        functools.partial(_ffn_kernel, tm=tm, nj=nj, lay=lay),
        grid=(t // tm, nj),
        in_specs=[
            pl.BlockSpec((tm, d), lambda i, j: (i, 0)),
            pl.BlockSpec(mod.shape, lambda i, j: (0, 0, 0)),
            pl.BlockSpec((1, d), lambda i, j: (0, 0)),
            pl.BlockSpec((d, tf), lambda i, j: (0, j)),
            pl.BlockSpec((d, tf), lambda i, j: (0, j)),
            pl.BlockSpec((tf, d), lambda i, j: (j, 0)),
        ],
        out_specs=pl.BlockSpec((tm, d), lambda i, j: (i, 0)),
        out_shape=jax.ShapeDtypeStruct((t, d), F32),
        scratch_shapes=[pltpu.VMEM((tm, d), BF16)],
        compiler_params=pltpu.CompilerParams(
            dimension_semantics=("parallel", "arbitrary"),
            vmem_limit_bytes=_vmem_limit(blocks, _nbytes((tm, d), BF16), temps)),
        name="ffn",
    )(x, mod, ng.reshape(1, d), w1, w3, w2)


def _inproj_kernel(x_ref, mod_ref, ng_ref, *refs, mode, has_bias, tm, lay):
    i = pl.program_id(0)
    j = pl.program_id(1)
    h_ref = refs[-1]

    @pl.when(j == 0)
    def _():
        _norm_mod_rows(h_ref, x_ref, mod_ref, ng_ref, i * tm, tm, lay)

    h = h_ref[...]
    if mode == "single":
        if has_bias:
            w_ref, b_ref, o_ref = refs[:3]
            o_ref[...] = _dot(h, w_ref[...]) + b_ref[...]
        else:
            w_ref, o_ref = refs[:2]
            o_ref[...] = _dot(h, w_ref[...])
    elif mode == "glu":
        wa_ref, wb_ref, ba_ref, bb_ref, o_ref = refs[:5]
        a = _dot(h, wa_ref[...]) + ba_ref[...]
        b = _dot(h, wb_ref[...]) + bb_ref[...]
        o_ref[...] = a * jax.nn.sigmoid(b)
    else:
        wa_ref, wb_ref, ba_ref, bb_ref, ou_ref, ov_ref = refs[:6]
        ou_ref[...] = jax.nn.gelu(_dot(h, wa_ref[...]) + ba_ref[...], approximate=True).astype(ou_ref.dtype)
        ov_ref[...] = jax.nn.gelu(_dot(h, wb_ref[...]) + bb_ref[...], approximate=True)


def _inproj(x, mod, ng, w, b, lay, mode):
    t, d = x.shape
    n = w.shape[1]
    tm = _pick(t, (1024, 768, 512, 384, 256, 128))
    tn = _pick(n if mode == "single" else n // 2, (512, 256, 128))
    x_spec = pl.BlockSpec((tm, d), lambda i, j: (i, 0))
    mod_spec = pl.BlockSpec(mod.shape, lambda i, j: (0, 0, 0))
    ng_spec = pl.BlockSpec((1, d), lambda i, j: (0, 0))
    lo = pl.BlockSpec((tm, tn), lambda i, j: (i, j))
    if mode == "single":
        nj = n // tn
        ins = [x, mod, ng.reshape(1, d), w]
        specs = [x_spec, mod_spec, ng_spec, pl.BlockSpec((d, tn), lambda i, j: (0, j))]
        if b is not None:
            ins.append(b.reshape(1, n))
            specs.append(pl.BlockSpec((1, tn), lambda i, j: (0, j)))
        out_shape = jax.ShapeDtypeStruct((t, n), F32)
        out_specs = lo
        n_w = 1
    else:
        half = n // 2
        nj = half // tn
        ins = [x, mod, ng.reshape(1, d), w, w, b.reshape(1, n), b.reshape(1, n)]
        specs = [x_spec, mod_spec, ng_spec,
                 pl.BlockSpec((d, tn), lambda i, j: (0, j)),
                 pl.BlockSpec((d, tn), lambda i, j: (0, j + nj)),
                 pl.BlockSpec((1, tn), lambda i, j: (0, j)),
                 pl.BlockSpec((1, tn), lambda i, j: (0, j + nj))]
        n_w = 2
        if mode == "glu":
            out_shape = jax.ShapeDtypeStruct((t, half), F32)
            out_specs = lo
        else:
            out_shape = (jax.ShapeDtypeStruct((t, half), BF16), jax.ShapeDtypeStruct((t, half), F32))
            out_specs = (lo, lo)
    blocks = (_nbytes((tm, d), F32) + _nbytes(mod.shape, F32) + n_w * _nbytes((d, tn), BF16)
              + n_w * _nbytes((tm, tn), F32))
    return pl.pallas_call(
        functools.partial(_inproj_kernel, mode=mode, has_bias=b is not None, tm=tm, lay=lay),
        grid=(t // tm, nj),
        in_specs=specs,
        out_specs=out_specs,
        out_shape=out_shape,
        scratch_shapes=[pltpu.VMEM((tm, d), BF16)],
        compiler_params=pltpu.CompilerParams(
            dimension_semantics=("parallel", "arbitrary"),
            vmem_limit_bytes=_vmem_limit(blocks, _nbytes((tm, d), BF16), 3 * n_w * _nbytes((tm, tn), F32))),
        name="inproj_" + mode,
    )(*ins)


def _outproj_kernel(x_ref, a_ref, mod_ref, w_ref, *refs, has_bias, tm, nk, lay):
    i = pl.program_id(0)
    k = pl.program_id(1)
    b_ref, o_ref = (refs[0], refs[1]) if has_bias else (None, refs[0])
    _accumulate(o_ref, _dot(a_ref[...], w_ref[...]), k)

    @pl.when(k == nk - 1)
    def _():
        _gated_residual_rows(o_ref, x_ref, mod_ref, b_ref, i * tm, tm, lay, 1.0)


def _outproj(x, a, mod, w, b, lay):
    t, d = x.shape
    kdim = a.shape[1]
    tm = _pick(t, (512, 384, 256, 128))
    tk = _pick(kdim, (2048, 1024, 512))
    nk = kdim // tk
    ins = [x, a, mod, w]
    specs = [pl.BlockSpec((tm, d), lambda i, k: (i, 0)),
             pl.BlockSpec((tm, tk), lambda i, k: (i, k)),
             pl.BlockSpec(mod.shape, lambda i, k: (0, 0, 0)),
             pl.BlockSpec((tk, d), lambda i, k: (k, 0))]
    if b is not None:
        ins.append(b.reshape(1, d))
        specs.append(pl.BlockSpec((1, d), lambda i, k: (0, 0)))
    blocks = (2 * _nbytes((tm, d), F32) + _nbytes((tm, tk), BF16) + _nbytes(mod.shape, F32)
              + _nbytes((tk, d), BF16))
    return pl.pallas_call(
        functools.partial(_outproj_kernel, has_bias=b is not None, tm=tm, nk=nk, lay=lay),
        grid=(t // tm, nk),
        in_specs=specs,
        out_specs=pl.BlockSpec((tm, d), lambda i, k: (i, 0)),
        out_shape=jax.ShapeDtypeStruct((t, d), F32),
        compiler_params=pltpu.CompilerParams(
            dimension_semantics=("parallel", "arbitrary"),
            vmem_limit_bytes=_vmem_limit(blocks, 0, _nbytes((tm, d), F32))),
        name="outproj",
    )(*ins)


def _conv_mid_kernel(g_ref, halo_ref, wdw_ref, bdw_ref, lng_ref, lnb_ref, o_ref, ext_ref, y_ref, *, tt, cb):
    d = g_ref.shape[-1]
    ext_ref[0:CONV_HALO, :] = halo_ref[...]
    ext_ref[CONV_HALO:CONV_HALO + tt, :] = g_ref[...]
    first = CONV_HALO - CONV_STATE
    for c0 in range(0, d, cb):
        acc = jnp.zeros((tt, cb), F32)
        for k in range(CONV_WIDTH):
            acc = acc + ext_ref[first + k:first + k + tt, c0:c0 + cb] * wdw_ref[k:k + 1, c0:c0 + cb]
        y_ref[:, c0:c0 + cb] = acc + bdw_ref[:, c0:c0 + cb]
    y = y_ref[...]
    yc = y - jnp.mean(y, axis=-1, keepdims=True)
    var = jnp.mean(yc * yc, axis=-1, keepdims=True)
    z = yc * lax.rsqrt(var + EPS) * lng_ref[...] + lnb_ref[...]
    o_ref[...] = (z * jax.nn.sigmoid(z)).astype(o_ref.dtype)


def _conv_mid(g_tiles, halo, w_dw, b_dw, ln_g, ln_b):
    n, tt, d = g_tiles.shape
    cb = 512
    blocks = (_nbytes((tt, d), F32) + _nbytes((CONV_HALO, d), F32) + _nbytes((32, d), F32)
              + 3 * _nbytes((8, d), F32) + _nbytes((tt, d), BF16))
    vec = pl.BlockSpec((1, d), lambda i: (0, 0))
    return pl.pallas_call(
        functools.partial(_conv_mid_kernel, tt=tt, cb=cb),
        grid=(n,),
        in_specs=[
            pl.BlockSpec((None, tt, d), lambda i: (i, 0, 0)),
            pl.BlockSpec((None, CONV_HALO, d), lambda i: (i, 0, 0)),
            pl.BlockSpec((CONV_WIDTH, d), lambda i: (0, 0)),
            vec, vec, vec,
        ],
        out_specs=pl.BlockSpec((None, tt, d), lambda i: (i, 0, 0)),
        out_shape=jax.ShapeDtypeStruct((n, tt, d), BF16),
        scratch_shapes=[pltpu.VMEM((CONV_HALO + tt, d), F32), pltpu.VMEM((tt, d), F32)],
        compiler_params=pltpu.CompilerParams(
            dimension_semantics=("parallel",),
            vmem_limit_bytes=_vmem_limit(blocks, _nbytes((CONV_HALO + 2 * tt, d), F32), 4 * _nbytes((tt, d), F32))),
        name="conv_mid",
    )(g_tiles, halo, w_dw, b_dw.reshape(1, d), ln_g.reshape(1, d), ln_b.reshape(1, d))


def _pool_mid_kernel(p_ref, halo_ref, wg_ref, sc_ref, o_ref, ext_ref, *, tt, n_prompt_tiles, tiles_per_seq):
    n = pl.program_id(0)
    d = p_ref.shape[-1]
    gw = d // len(POOL_WINDOWS)
    ext_ref[0:POOL_HALO, :] = halo_ref[...]
    ext_ref[POOL_HALO:POOL_HALO + tt, :] = p_ref[...]
    pos0 = jnp.where(n < n_prompt_tiles, (n % tiles_per_seq) * tt, PAST_LEN)
    pos = pos0 + lax.broadcasted_iota(jnp.int32, (tt, 1), 0)
    for gi, w in enumerate(POOL_WINDOWS):
        c0 = gi * gw
        s = ext_ref[POOL_HALO:POOL_HALO + tt, c0:c0 + gw]
        for back in range(1, w):
            s = s + ext_ref[POOL_HALO - back:POOL_HALO - back + tt, c0:c0 + gw]
        cnt = jnp.minimum(pos + 1, w).astype(F32)
        pooled = s / cnt - p_ref[:, c0:c0 + gw]
        z = _dot(pooled.astype(BF16), wg_ref[gi]) * sc_ref[:, c0:c0 + gw]
        o_ref[:, c0:c0 + gw] = z.astype(o_ref.dtype)


def _pool_mid(p_tiles, halo, w_grp, scale, n_prompt_tiles, tiles_per_seq):
    n, tt, d = p_tiles.shape
    blocks = (_nbytes((tt, d), F32) + _nbytes((POOL_HALO, d), F32) + _nbytes(w_grp.shape, BF16)
              + _nbytes((8, d), F32) + _nbytes((tt, d), BF16))
    return pl.pallas_call(
        functools.partial(_pool_mid_kernel, tt=tt, n_prompt_tiles=n_prompt_tiles, tiles_per_seq=tiles_per_seq),
        grid=(n,),
        in_specs=[
            pl.BlockSpec((None, tt, d), lambda i: (i, 0, 0)),
            pl.BlockSpec((None, POOL_HALO, d), lambda i: (i, 0, 0)),
            pl.BlockSpec(w_grp.shape, lambda i: (0, 0, 0)),
            pl.BlockSpec((1, d), lambda i: (0, 0)),
        ],
        out_specs=pl.BlockSpec((None, tt, d), lambda i: (i, 0, 0)),
        out_shape=jax.ShapeDtypeStruct((n, tt, d), BF16),
        scratch_shapes=[pltpu.VMEM((POOL_HALO + tt, d), F32)],
        compiler_params=pltpu.CompilerParams(
            dimension_semantics=("parallel",),
            vmem_limit_bytes=_vmem_limit(blocks, _nbytes((POOL_HALO + tt, d), F32), 4 * _nbytes((tt, d), F32))),
        name="pool_mid",
    )(p_tiles, halo, w_grp, scale.reshape(1, d))


def _attn_kernel(q_ref, k0_ref, k1_ref, k2_ref, v0_ref, v1_ref, v2_ref, bias_ref, sink_ref, o_ref, *,
                 n_prompt_chunks, chunks_per_seq):
    c = pl.program_id(0)
    n_hist = SWA_BAND // CHUNK
    n_eff = jnp.where(c < n_prompt_chunks, jnp.minimum(c % chunks_per_seq, n_hist), n_hist)
    first_valid = (n_hist - n_eff) * CHUNK
    rows = Q_PER_KV * CHUNK
    keys = SWA_BAND + CHUNK
    valid = lax.broadcasted_iota(jnp.int32, (rows, keys), 1) >= first_valid
    q = q_ref[...].astype(BF16)
    for kh in range(N_KV_HEADS):
        cols = slice(kh * HEAD_DIM, (kh + 1) * HEAD_DIM)
        kcat = jnp.concatenate([k0_ref[:, cols], k1_ref[:, cols], k2_ref[:, cols]], axis=0).astype(BF16)
        vcat = jnp.concatenate([v0_ref[:, cols], v1_ref[:, cols], v2_ref[:, cols]], axis=0).astype(BF16)
        qs = jnp.concatenate(
            [q[:, (kh * Q_PER_KV + g) * HEAD_DIM:(kh * Q_PER_KV + g + 1) * HEAD_DIM] for g in range(Q_PER_KV)],
            axis=0)
        s = lax.dot_general(qs, kcat, (((1,), (1,)), ((), ())), preferred_element_type=F32)
        s = s * ATTN_SCALE + bias_ref[kh]
        s = jnp.where(valid, s, -jnp.inf)
        sk = sink_ref[kh]
        m = jnp.maximum(jnp.max(s, axis=-1, keepdims=True), sk)
        e = jnp.exp(s - m)
        p = e / (jnp.sum(e, axis=-1, keepdims=True) + jnp.exp(sk - m))
        o = _dot(p.astype(BF16), vcat)
        for g in range(Q_PER_KV):
            h0 = (kh * Q_PER_KV + g) * HEAD_DIM
            o_ref[:, h0:h0 + HEAD_DIM] = o[g * CHUNK:(g + 1) * CHUNK, :].astype(o_ref.dtype)


def _attention(qkv, kpad, vpad, bias, sinks, lay):
    t = qkv.shape[0]
    hw = N_HEADS * HEAD_DIM
    kvw = N_KV_HEADS * HEAD_DIM
    n_hist = SWA_BAND // CHUNK
    cps = lay.sp // CHUNK
    n_prompt_chunks = lay.bp * cps

    def kv_map(off):
        def index(c):
            base = jnp.where(c < n_prompt_chunks,
                             (c // cps) * (cps + n_hist) + c % cps,
                             lay.bp * (cps + n_hist) + (c - n_prompt_chunks) * (n_hist + 1))
            return (base + off, 0)
        return index

    kv_specs = [pl.BlockSpec((CHUNK, kvw), kv_map(o)) for o in range(n_hist + 1)]
    rows = Q_PER_KV * CHUNK
    keys = SWA_BAND + CHUNK
    blocks = (_nbytes((CHUNK, hw), F32) + 6 * _nbytes((CHUNK, kvw), F32) + _nbytes(bias.shape, F32)
              + _nbytes((N_KV_HEADS, rows, 128), F32) + _nbytes((CHUNK, hw), BF16))
    return pl.pallas_call(
        functools.partial(_attn_kernel, n_prompt_chunks=n_prompt_chunks, chunks_per_seq=cps),
        grid=(t // CHUNK,),
        in_specs=[pl.BlockSpec((CHUNK, hw), lambda c: (c, 0))] + kv_specs + kv_specs + [
            pl.BlockSpec(bias.shape, lambda c: (0, 0, 0)),
            pl.BlockSpec(sinks.shape, lambda c: (0, 0, 0)),
        ],
        out_specs=pl.BlockSpec((CHUNK, hw), lambda c: (c, 0)),
        out_shape=jax.ShapeDtypeStruct((t, hw), BF16),
        compiler_params=pltpu.CompilerParams(
            dimension_semantics=("parallel",),
            vmem_limit_bytes=_vmem_limit(blocks, 0, 8 * _nbytes((rows, keys + 64), F32))),
        name="swa_attention",
    )(qkv, kpad, kpad, kpad, vpad, vpad, vpad, bias, sinks)


def _gmlp_mid_kernel(u_ref, v_ref, lng_ref, lnb_ref, ws_ref, bs_ref, o_ref, vn_ref, *, n_prompt_blocks):
    n = pl.program_id(0)
    is_sample = n >= n_prompt_blocks
    v = v_ref[...]
    vc = v - jnp.mean(v, axis=-1, keepdims=True)
    var = jnp.mean(vc * vc, axis=-1, keepdims=True)
    vn = vc * lax.rsqrt(var + EPS) * lng_ref[...] + lnb_ref[...]
    vn_ref[...] = vn
    vnb = vn.astype(BF16)
    ll = GMLP_CHUNK
    rb = lax.broadcasted_iota(jnp.int32, (ll, ll), 0) // CHUNK
    cb = lax.broadcasted_iota(jnp.int32, (ll, ll), 1) // CHUNK
    back = rb - cb
    mask = (back >= 0) & (back <= jnp.where(is_sample, 0, 1))
    gw = u_ref.shape[-1] // GMLP_GROUPS
    for g in range(GMLP_GROUPS):
        w = jnp.where(mask, ws_ref[g], 0.0).astype(BF16)
        mixed = _dot(w, vnb[:, g * gw:(g + 1) * gw]) + bs_ref[g]
        o_ref[:, g * gw:(g + 1) * gw] = (u_ref[:, g * gw:(g + 1) * gw].astype(F32) * mixed).astype(o_ref.dtype)


def _gmlp_mid(u, v, ln_g, ln_b, ws2, bs2, n_prompt_blocks):
    t, gwid = u.shape
    ll = GMLP_CHUNK
    sel = lambda n: jnp.where(n >= n_prompt_blocks, 1, 0)
    blocks = (_nbytes((ll, gwid), BF16) + 2 * _nbytes((ll, gwid), F32) + 2 * _nbytes((8, gwid), F32)
              + _nbytes((GMLP_GROUPS, ll, ll), F32) + _nbytes((GMLP_GROUPS, ll, 128), F32)
              + _nbytes((ll, gwid), BF16))
    return pl.pallas_call(
        functools.partial(_gmlp_mid_kernel, n_prompt_blocks=n_prompt_blocks),
        grid=(t // ll,),
        in_specs=[
            pl.BlockSpec((ll, gwid), lambda n: (n, 0)),
            pl.BlockSpec((ll, gwid), lambda n: (n, 0)),
            pl.BlockSpec((1, gwid), lambda n: (0, 0)),
            pl.BlockSpec((1, gwid), lambda n: (0, 0)),
            pl.BlockSpec((None, GMLP_GROUPS, ll, ll), lambda n: (sel(n), 0, 0, 0)),
            pl.BlockSpec((None, GMLP_GROUPS, ll, 1), lambda n: (sel(n), 0, 0, 0)),
        ],
        out_specs=(pl.BlockSpec((ll, gwid), lambda n: (n, 0)), pl.BlockSpec((ll, gwid), lambda n: (n, 0))),
        out_shape=(jax.ShapeDtypeStruct((t, gwid), BF16), jax.ShapeDtypeStruct((t, gwid), F32)),
        compiler_params=pltpu.CompilerParams(
            dimension_semantics=("parallel",),
            vmem_limit_bytes=_vmem_limit(blocks, 0, 6 * _nbytes((ll, gwid), F32))),
        name="gmlp_mid",
    )(u, v, ln_g.reshape(1, gwid), ln_b.reshape(1, gwid), ws2, bs2)


def _final_norm_kernel(x_ref, g_ref, o_ref):
    x = x_ref[...]
    ms = jnp.mean(x * x, axis=-1, keepdims=True)
    o_ref[...] = x * lax.rsqrt(ms + EPS) * g_ref[...]


def _final_norm(x, g):
    t, d = x.shape
    tm = _pick(t, (256, 128))
    return pl.pallas_call(
        _final_norm_kernel,
        grid=(t // tm,),
        in_specs=[pl.BlockSpec((tm, d), lambda i: (i, 0)), pl.BlockSpec((1, d), lambda i: (0, 0))],
        out_specs=pl.BlockSpec((tm, d), lambda i: (i, 0)),
        out_shape=jax.ShapeDtypeStruct((t, d), F32),
        compiler_params=pltpu.CompilerParams(
            dimension_semantics=("parallel",),
            vmem_limit_bytes=_vmem_limit(2 * _nbytes((tm, d), F32), 0, 2 * _nbytes((tm, d), F32))),
        name="final_norm",
    )(x, g.reshape(1, d))


def _halo_tiles(y, cache, halo_rows, lay):
    d = y.shape[-1]
    tps = lay.sp // CHUNK
    yp = y[:lay.tp].reshape(lay.bp, tps, CHUNK, d)
    hp = jnp.pad(yp[:, :-1, CHUNK - halo_rows:, :], ((0, 0), (1, 0), (0, 0), (0, 0)))
    hs = jnp.pad(cache, ((0, 0), (halo_rows - cache.shape[1], 0), (0, 0)))
    return jnp.concatenate([hp.reshape(lay.bp * tps, halo_rows, d), hs], axis=0)


def _state_tail(y, cache, keep, lay):
    d = y.shape[-1]
    yp = y[:lay.tp].reshape(lay.bp, lay.sp, d)
    ys = y[lay.tp:].reshape(lay.bs, lay.ts, d)
    return yp[:, lay.sp - keep:], jnp.concatenate([cache, ys], axis=1)[:, -keep:]


def kernel(x_prompt, x_sample, c_prompt, c_sample, cache_conv, cache_pool, cache_swa_k, cache_swa_v, ada_w, ada_b, norm_g, ffn_w1, ffn_w3, ffn_w2, conv_w_pw1, conv_b_pw1, conv_w_dw, conv_b_dw, conv_ln_g, conv_ln_b, conv_w_pw2, conv_b_pw2, pool_w_in, pool_w_grp, pool_scale, pool_w_out, swa_wq, swa_wk, swa_wv, swa_wo, swa_sinks, gmlp_w_in, gmlp_b_in, gmlp_ln_g, gmlp_ln_b, gmlp_w_s, gmlp_b_s, gmlp_w_out, gmlp_b_out, final_g):
    bp, sp, d = x_prompt.shape
    bs, ts, _ = x_sample.shape
    lay = Layout(bp, sp, bs, ts)
    depth = ada_w.shape[0]
    kvw = N_KV_HEADS * HEAD_DIM
    hw = N_HEADS * HEAD_DIM

    x = jnp.concatenate([x_prompt.reshape(lay.tp, d), x_sample.reshape(lay.tsamp, d)], axis=0)
    c_all = jnp.pad(jnp.concatenate([c_prompt, c_sample], axis=0), ((0, SEQ_PAD - bp - bs), (0, 0)))
    mod = _ada_table(c_all, ada_w, ada_b).reshape(depth, SEQ_PAD, 3, 3, d)

    w1b, w3b, w2b = ffn_w1.astype(BF16), ffn_w3.astype(BF16), ffn_w2.astype(BF16)

    outs = {}
    for i in range(depth):
        mod_i = [mod[i, :, s] for s in range(3)]
        x = _ffn(x, mod_i[0], norm_g[i, 0], w1b[i, 0], w3b[i, 0], w2b[i, 0], lay)
        kind = i % 4
        if kind == 0:
            g = _inproj(x, mod_i[1], norm_g[i, 1], conv_w_pw1.astype(BF16), conv_b_pw1, lay, "glu")
            halo = _halo_tiles(g, cache_conv, CONV_HALO, lay)
            a = _conv_mid(g.reshape(lay.t // CHUNK, CHUNK, d), halo, conv_w_dw, conv_b_dw, conv_ln_g, conv_ln_b)
            x = _outproj(x, a.reshape(lay.t, d), mod_i[1], conv_w_pw2.astype(BF16), conv_b_pw2, lay)
            outs["conv"] = _state_tail(g, cache_conv, CONV_STATE, lay)
        elif kind == 1:
            p = _inproj(x, mod_i[1], norm_g[i, 1], pool_w_in.astype(BF16), None, lay, "single")
            halo = _halo_tiles(p, cache_pool, POOL_HALO, lay)
            a = _pool_mid(p.reshape(lay.t // CHUNK, CHUNK, d), halo, pool_w_grp.astype(BF16), pool_scale,
                          lay.tp // CHUNK, lay.sp // CHUNK)
            x = _outproj(x, a.reshape(lay.t, d), mod_i[1], pool_w_out.astype(BF16), None, lay)
            outs["pool"] = _state_tail(p, cache_pool, POOL_STATE, lay)
        elif kind == 2:
            w_qkv = jnp.concatenate([swa_wq, swa_wk, swa_wv], axis=1).astype(BF16)
            qkv = _inproj(x, mod_i[1], norm_g[i, 1], w_qkv, None, lay, "single")
            r = cache_swa_k.shape[1]
            assert r == SWA_BAND

            def padded(col0, cache):
                new = qkv[:, col0:col0 + kvw]
                newp = jnp.pad(new[:lay.tp].reshape(bp, sp, kvw), ((0, 0), (SWA_BAND, 0), (0, 0)))
                news = jnp.concatenate([cache.reshape(bs, r, kvw), new[lay.tp:].reshape(bs, ts, kvw)], axis=1)
                tail_p = new[:lay.tp].reshape(bp, sp, kvw)[:, sp - SWA_BAND:]
                tail_s = news[:, ts:]
                both = jnp.concatenate([newp.reshape(-1, kvw), news.reshape(-1, kvw)], axis=0)
                return both, tail_p.reshape(bp, SWA_BAND, N_KV_HEADS, HEAD_DIM), tail_s.reshape(bs, r, N_KV_HEADS, HEAD_DIM)

            kpad, k_p, k_s = padded(hw, cache_swa_k)
            vpad, v_p, v_s = padded(hw + kvw, cache_swa_v)
            slopes = jnp.exp2(-8.0 * jnp.arange(1, N_HEADS + 1, dtype=F32) / N_HEADS).reshape(N_KV_HEADS, Q_PER_KV)
            dist = jnp.abs(jnp.arange(CHUNK)[:, None] + SWA_BAND - jnp.arange(SWA_BAND + CHUNK)[None, :]).astype(F32)
            bias = (-slopes[:, :, None, None] * dist[None, None]).reshape(N_KV_HEADS, Q_PER_KV * CHUNK, SWA_BAND + CHUNK)
            sinks = jnp.broadcast_to(swa_sinks.astype(F32).reshape(N_KV_HEADS, Q_PER_KV, 1, 1),
                                     (N_KV_HEADS, Q_PER_KV, CHUNK, 1)).reshape(N_KV_HEADS, Q_PER_KV * CHUNK, 1)
            a = _attention(qkv, kpad, vpad, bias, sinks, lay)
            x = _outproj(x, a, mod_i[1], swa_wo.astype(BF16), None, lay)
            outs["kv"] = (k_p, v_p, k_s, v_s)
        else:
            u, v = _inproj(x, mod_i[1], norm_g[i, 1], gmlp_w_in.astype(BF16), gmlp_b_in, lay, "gelu2")
            ll = GMLP_CHUNK
            ws2 = jnp.stack([gmlp_w_s[:, :ll, :ll], jnp.tile(gmlp_w_s[:, :ts, :ts], (1, ll // ts, ll // ts))])
            bs2 = jnp.stack([gmlp_b_s[:, :ll], jnp.tile(gmlp_b_s[:, :ts], (1, ll // ts))])[..., None]
            a, vn = _gmlp_mid(u, v, gmlp_ln_g, gmlp_ln_b, ws2, bs2, lay.tp // ll)
            x = _outproj(x, a, mod_i[1], gmlp_w_out.astype(BF16), gmlp_b_out, lay)
            outs["gmlp_v"] = vn[lay.tp:].reshape(bs, ts, -1)
        x = _ffn(x, mod_i[2], norm_g[i, 2], w1b[i, 1], w3b[i, 1], w2b[i, 1], lay)

    y = _final_norm(x, final_g)
    y_prompt = y[:lay.tp].reshape(bp, sp, d)
    y_sample = y[lay.tp:].reshape(bs, ts, d)
    conv_p, conv_s = outs["conv"]
    pool_p, pool_s = outs["pool"]
    k_p, v_p, k_s, v_s = outs["kv"]
    return (y_prompt, y_sample, conv_p, conv_s, pool_p, pool_s, k_p, v_p, k_s, v_s, outs["gmlp_v"])
```

```python
import functools

import jax
import jax.numpy as jnp
from jax import lax
from jax.experimental import pallas as pl
from jax.experimental.pallas import tpu as pltpu

F32 = jnp.float32
BF16 = jnp.bfloat16

EPS = 1e-6
MACARON = 0.5
N_MOD = 9
CONV_WIDTH = 31
CONV_STATE = CONV_WIDTH - 1
POOL_WINDOWS = (2, 4, 8, 16)
POOL_STATE = max(POOL_WINDOWS) - 1
HEAD_DIM = 64
N_KV_HEADS = 4
Q_PER_KV = 8
N_HEADS = N_KV_HEADS * Q_PER_KV
CHUNK = 64
SWA_BAND = 128
ATTN_SCALE = HEAD_DIM ** -0.5
GMLP_CHUNK = 128
GMLP_GROUPS = 8
PAST_LEN = 1024

LANES = 128
SUBLANES = 8
ROW_CHUNK = 16
CONV_HALO = 32
POOL_HALO = 16
SEQ_PAD = 32

V7X_VMEM_LIMIT_CAP = 60 * 1024 * 1024


class Layout:
    def __init__(self, bp, sp, bs, ts):
        self.bp, self.sp, self.bs, self.ts = bp, sp, bs, ts
        self.tp = bp * sp
        self.tsamp = bs * ts
        self.t = self.tp + self.tsamp
        assert sp % GMLP_CHUNK == 0 and ts == CHUNK and self.tsamp % GMLP_CHUNK == 0
        assert bp + bs <= SEQ_PAD

    def seq_of(self, row0):
        return jnp.where(row0 < self.tp, row0 // self.sp, self.bp + (row0 - self.tp) // self.ts)


def _pick(total, candidates):
    for c in candidates:
        if total % c == 0:
            return c
    raise ValueError(f"no tile for {total} in {candidates}")


def _vmem_limit(pipelined_bytes, scratch_bytes=0, temp_bytes=0):
    est = 2 * pipelined_bytes + scratch_bytes + temp_bytes + (4 << 20)
    return int(min(max(est, 16 << 20), V7X_VMEM_LIMIT_CAP))


def _nbytes(shape, dtype):
    n = 1
    for s in shape:
        n *= s
    return n * jnp.dtype(dtype).itemsize


def _dot(a, b):
    return jnp.dot(a, b, preferred_element_type=F32)


def _norm_mod_rows(h_ref, x_ref, mod_ref, ng_ref, row_base, tm, lay):
    ng = ng_ref[...]

    def body(c, carry):
        r = pl.multiple_of(c * CHUNK, CHUNK)
        seq = lay.seq_of(row_base + r)
        shift = mod_ref[0, pl.ds(seq, 1), :]
        scale1 = 1.0 + mod_ref[1, pl.ds(seq, 1), :]
        for s in range(CHUNK // ROW_CHUNK):
            rows = pl.ds(r + s * ROW_CHUNK, ROW_CHUNK)
            xg = x_ref[rows, :]
            ms = jnp.mean(xg * xg, axis=-1, keepdims=True)
            y = xg * lax.rsqrt(ms + EPS) * ng
            h_ref[rows, :] = (y * scale1 + shift).astype(h_ref.dtype)
        return carry

    lax.fori_loop(0, tm // CHUNK, body, 0)


def _gated_residual_rows(o_ref, x_ref, mod_ref, b_ref, row_base, tm, lay, factor):
    def body(c, carry):
        r = pl.multiple_of(c * CHUNK, CHUNK)
        gate = factor * mod_ref[2, pl.ds(lay.seq_of(row_base + r), 1), :]
        for s in range(CHUNK // ROW_CHUNK):
            rows = pl.ds(r + s * ROW_CHUNK, ROW_CHUNK)
            acc = o_ref[rows, :]
            if b_ref is not None:
                acc = acc + b_ref[...]
            o_ref[rows, :] = x_ref[rows, :] + gate * acc
        return carry

    lax.fori_loop(0, tm // CHUNK, body, 0)


def _mod_spec(mod, layer, sub, grid_rank):
    zeros = (0,) * 3
    block = (None, None) + mod.shape[2:]
    if grid_rank == 1:
        return pl.BlockSpec(block, lambda i: (layer, sub) + zeros)
    return pl.BlockSpec(block, lambda i, j: (layer, sub) + zeros)


def _norm_g_spec(norm_g4, layer, sub):
    return pl.BlockSpec((None, None, 1, norm_g4.shape[-1]), lambda i, j: (layer, sub, 0, 0))


def _ada_kernel(c_ref, w_ref, b_ref, o_ref):
    c = c_ref[...]
    cond = (c * jax.nn.sigmoid(c)).astype(BF16)
    o_ref[...] = _dot(cond, w_ref[...].astype(BF16)) + b_ref[...]


def _ada_table(c_all, ada_w, ada_b):
    depth, d, n = ada_w.shape
    n_mod = n // d
    blocks = 2 * _nbytes((SEQ_PAD, d), F32) + _nbytes((d, d), F32) + _nbytes((SUBLANES, d), F32)
    return pl.pallas_call(
        _ada_kernel,
        grid=(depth, n_mod),
        in_specs=[
            pl.BlockSpec((SEQ_PAD, d), lambda l, j: (0, 0)),
            pl.BlockSpec((None, d, d), lambda l, j: (l, 0, j)),
            pl.BlockSpec((None, None, 1, d), lambda l, j: (l, j, 0, 0)),
        ],
        out_specs=pl.BlockSpec((None, None, SEQ_PAD, d), lambda l, j: (l, j, 0, 0)),
        out_shape=jax.ShapeDtypeStruct((depth, n_mod, SEQ_PAD, d), F32),
        compiler_params=pltpu.CompilerParams(
            dimension_semantics=("parallel", "arbitrary"),
            vmem_limit_bytes=_vmem_limit(blocks, temp_bytes=_nbytes((d, d), BF16))),
        name="ada_table",
    )(c_all, ada_w, ada_b.reshape(depth, n_mod, 1, d))


def _ffn_kernel(x_ref, mod_ref, ng_ref, w1_ref, w3_ref, w2_ref, o_ref, h_ref, *, tm, nj, lay):
    i = pl.program_id(0)
    j = pl.program_id(1)

    @pl.when(j == 0)
    def _():
        _norm_mod_rows(h_ref, x_ref, mod_ref, ng_ref, i * tm, tm, lay)
        o_ref[...] = jnp.zeros_like(o_ref)

    h = h_ref[...]
    a = _dot(h, w1_ref[...])
    b = _dot(h, w3_ref[...])
    g = (a * jax.nn.sigmoid(a) * b).astype(BF16)
    o_ref[...] += _dot(g, w2_ref[...])

    @pl.when(j == nj - 1)
    def _():
        _gated_residual_rows(o_ref, x_ref, mod_ref, None, i * tm, tm, lay, MACARON)


def _ffn(x, mod, norm_g4, w1, w3, w2, layer, sub, which, lay):
    t, d = x.shape
    f = w1.shape[-1]
    tm = _pick(t, (768, 512, 384, 256, 128))
    tf = _pick(f, (512, 256, 128))
    nj = f // tf
    blocks = (2 * _nbytes((tm, d), F32) + _nbytes(mod.shape[2:], F32) + 2 * _nbytes((d, tf), BF16)
              + _nbytes((tf, d), BF16))
    temps = 2 * _nbytes((tm, tf), F32) + _nbytes((tm, tf), BF16)
    return pl.pallas_call(
        functools.partial(_ffn_kernel, tm=tm, nj=nj, lay=lay),
        grid=(t // tm, nj),
        in_specs=[
            pl.BlockSpec((tm, d), lambda i, j: (i, 0)),
            _mod_spec(mod, layer, sub, 2),
            _norm_g_spec(norm_g4, layer, sub),
            pl.BlockSpec((None, None, d, tf), lambda i, j: (layer, which, 0, j)),
            pl.BlockSpec((None, None, d, tf), lambda i, j: (layer, which, 0, j)),
            pl.BlockSpec((None, None, tf, d), lambda i, j: (layer, which, j, 0)),
        ],
        out_specs=pl.BlockSpec((tm, d), lambda i, j: (i, 0)),
        out_shape=jax.ShapeDtypeStruct((t, d), F32),
        scratch_shapes=[pltpu.VMEM((tm, d), BF16)],
        compiler_params=pltpu.CompilerParams(
            dimension_semantics=("parallel", "arbitrary"),
            vmem_limit_bytes=_vmem_limit(blocks, _nbytes((tm, d), BF16), temps)),
        name="ffn",
    )(x, mod, norm_g4, w1, w3, w2)


def _inproj_kernel(x_ref, mod_ref, ng_ref, *refs, mode, has_bias, tm, lay):
    i = pl.program_id(0)
    j = pl.program_id(1)
    h_ref = refs[-1]

    @pl.when(j == 0)
    def _():
        _norm_mod_rows(h_ref, x_ref, mod_ref, ng_ref, i * tm, tm, lay)

    h = h_ref[...]
    if mode == "single":
        if has_bias:
            w_ref, b_ref, o_ref = refs[:3]
            o_ref[...] = _dot(h, w_ref[...].astype(BF16)) + b_ref[...]
        else:
            w_ref, o_ref = refs[:2]
            o_ref[...] = _dot(h, w_ref[...].astype(BF16))
    elif mode == "glu":
        wa_ref, wb_ref, ba_ref, bb_ref, o_ref = refs[:5]
        a = _dot(h, wa_ref[...].astype(BF16)) + ba_ref[...]
        b = _dot(h, wb_ref[...].astype(BF16)) + bb_ref[...]
        o_ref[...] = a * jax.nn.sigmoid(b)
    else:
        wa_ref, wb_ref, ba_ref, bb_ref, ou_ref, ov_ref = refs[:6]
        ou_ref[...] = jax.nn.gelu(_dot(h, wa_ref[...].astype(BF16)) + ba_ref[...],
                                  approximate=True).astype(ou_ref.dtype)
        ov_ref[...] = jax.nn.gelu(_dot(h, wb_ref[...].astype(BF16)) + bb_ref[...], approximate=True)


def _inproj(x, mod, norm_g4, layer, w, b, lay, mode):
    t, d = x.shape
    n = w.shape[1]
    tm = _pick(t, (1024, 768, 512, 384, 256, 128))
    tn = _pick(n if mode == "single" else n // 2, (512, 256, 128))
    x_spec = pl.BlockSpec((tm, d), lambda i, j: (i, 0))
    mod_spec = _mod_spec(mod, layer, 1, 2)
    ng_spec = _norm_g_spec(norm_g4, layer, 1)
    lo = pl.BlockSpec((tm, tn), lambda i, j: (i, j))
    if mode == "single":
        nj = n // tn
        ins = [x, mod, norm_g4, w]
        specs = [x_spec, mod_spec, ng_spec, pl.BlockSpec((d, tn), lambda i, j: (0, j))]
        if b is not None:
            ins.append(b.reshape(1, n))
            specs.append(pl.BlockSpec((1, tn), lambda i, j: (0, j)))
        out_shape = jax.ShapeDtypeStruct((t, n), F32)
        out_specs = lo
        n_w = 1
    else:
        half = n // 2
        nj = half // tn
        ins = [x, mod, norm_g4, w, w, b.reshape(1, n), b.reshape(1, n)]
        specs = [x_spec, mod_spec, ng_spec,
                 pl.BlockSpec((d, tn), lambda i, j: (0, j)),
                 pl.BlockSpec((d, tn), lambda i, j: (0, j + nj)),
                 pl.BlockSpec((1, tn), lambda i, j: (0, j)),
                 pl.BlockSpec((1, tn), lambda i, j: (0, j + nj))]
        n_w = 2
        if mode == "glu":
            out_shape = jax.ShapeDtypeStruct((t, half), F32)
            out_specs = lo
        else:
            out_shape = (jax.ShapeDtypeStruct((t, half), BF16), jax.ShapeDtypeStruct((t, half), F32))
            out_specs = (lo, lo)
    blocks = (_nbytes((tm, d), F32) + _nbytes(mod.shape[2:], F32) + n_w * _nbytes((d, tn), w.dtype)
              + n_w * _nbytes((tm, tn), F32))
    return pl.pallas_call(
        functools.partial(_inproj_kernel, mode=mode, has_bias=b is not None, tm=tm, lay=lay),
        grid=(t // tm, nj),
        in_specs=specs,
        out_specs=out_specs,
        out_shape=out_shape,
        scratch_shapes=[pltpu.VMEM((tm, d), BF16)],
        compiler_params=pltpu.CompilerParams(
            dimension_semantics=("parallel", "arbitrary"),
            vmem_limit_bytes=_vmem_limit(blocks, _nbytes((tm, d), BF16),
                                         n_w * (3 * _nbytes((tm, tn), F32) + _nbytes((d, tn), BF16)))),
        name="inproj_" + mode,
    )(*ins)


def _outproj_kernel(x_ref, a_ref, mod_ref, w_ref, *refs, has_bias, tm, nk, lay):
    i = pl.program_id(0)
    k = pl.program_id(1)
    b_ref, o_ref = (refs[0], refs[1]) if has_bias else (None, refs[0])

    @pl.when(k == 0)
    def _():
        o_ref[...] = jnp.zeros_like(o_ref)

    o_ref[...] += _dot(a_ref[...], w_ref[...])

    @pl.when(k == nk - 1)
    def _():
        _gated_residual_rows(o_ref, x_ref, mod_ref, b_ref, i * tm, tm, lay, 1.0)


def _outproj(x, a, mod, layer, w, b, lay):
    t, d = x.shape
    kdim = a.shape[1]
    tm = _pick(t, (512, 384, 256, 128))
    tk = _pick(kdim, (2048, 1024, 512))
    nk = kdim // tk
    ins = [x, a, mod, w]
    specs = [pl.BlockSpec((tm, d), lambda i, k: (i, 0)),
             pl.BlockSpec((tm, tk), lambda i, k: (i, k)),
             _mod_spec(mod, layer, 1, 2),
             pl.BlockSpec((tk, d), lambda i, k: (k, 0))]
    if b is not None:
        ins.append(b.reshape(1, d))
        specs.append(pl.BlockSpec((1, d), lambda i, k: (0, 0)))
    blocks = (2 * _nbytes((tm, d), F32) + _nbytes((tm, tk), BF16) + _nbytes(mod.shape[2:], F32)
              + _nbytes((tk, d), BF16))
    return pl.pallas_call(
        functools.partial(_outproj_kernel, has_bias=b is not None, tm=tm, nk=nk, lay=lay),
        grid=(t // tm, nk),
        in_specs=specs,
        out_specs=pl.BlockSpec((tm, d), lambda i, k: (i, 0)),
        out_shape=jax.ShapeDtypeStruct((t, d), F32),
        compiler_params=pltpu.CompilerParams(
            dimension_semantics=("parallel", "arbitrary"),
            vmem_limit_bytes=_vmem_limit(blocks)),
        name="outproj",
    )(*ins)


def _conv_mid_kernel(g_ref, halo_ref, wdw_ref, bdw_ref, lng_ref, lnb_ref, o_ref, ext_ref, y_ref, *, tt, cb):
    d = g_ref.shape[-1]
    ext_ref[0:CONV_HALO, :] = halo_ref[...]
    ext_ref[CONV_HALO:CONV_HALO + tt, :] = g_ref[...]
    ext_ref[CONV_HALO + tt:CONV_HALO + tt + SUBLANES, :] = jnp.zeros((SUBLANES, d), F32)
    first = CONV_HALO - CONV_STATE
    for c0 in range(0, d, cb):
        cols = slice(c0, c0 + cb)
        y = None
        for r in range(SUBLANES):
            u = None
            for k in range(CONV_WIDTH):
                if (first + k) % SUBLANES != r:
                    continue
                a0 = (first + k) - r
                term = ext_ref[a0:a0 + tt + SUBLANES, cols] * wdw_ref[k:k + 1, cols]
                u = term if u is None else u + term
            if u is None:
                continue
            piece = u[r:r + tt, :]
            y = piece if y is None else y + piece
        y_ref[:, cols] = y + bdw_ref[:, cols]
    y = y_ref[...]
    yc = y - jnp.mean(y, axis=-1, keepdims=True)
    var = jnp.mean(yc * yc, axis=-1, keepdims=True)
    z = yc * lax.rsqrt(var + EPS) * lng_ref[...] + lnb_ref[...]
    o_ref[...] = (z * jax.nn.sigmoid(z)).astype(o_ref.dtype)


def _conv_mid(g_tiles, halo, w_dw, b_dw, ln_g, ln_b):
    n, tt, d = g_tiles.shape
    cb = 2 * LANES
    blocks = (_nbytes((tt, d), F32) + _nbytes((CONV_HALO, d), F32) + _nbytes((32, d), F32)
              + 3 * _nbytes((SUBLANES, d), F32) + _nbytes((tt, d), BF16))
    vec = pl.BlockSpec((1, d), lambda i: (0, 0))
    ext_rows = CONV_HALO + tt + SUBLANES
    return pl.pallas_call(
        functools.partial(_conv_mid_kernel, tt=tt, cb=cb),
        grid=(n,),
        in_specs=[
            pl.BlockSpec((None, tt, d), lambda i: (i, 0, 0)),
            pl.BlockSpec((None, CONV_HALO, d), lambda i: (i, 0, 0)),
            pl.BlockSpec((CONV_WIDTH, d), lambda i: (0, 0)),
            vec, vec, vec,
        ],
        out_specs=pl.BlockSpec((None, tt, d), lambda i: (i, 0, 0)),
        out_shape=jax.ShapeDtypeStruct((n, tt, d), BF16),
        scratch_shapes=[pltpu.VMEM((ext_rows, d), F32), pltpu.VMEM((tt, d), F32)],
        compiler_params=pltpu.CompilerParams(
            dimension_semantics=("parallel",),
            vmem_limit_bytes=_vmem_limit(blocks, _nbytes((ext_rows + tt, d), F32), 4 * _nbytes((tt, d), F32))),
        name="conv_mid",
    )(g_tiles, halo, w_dw, b_dw.reshape(1, d), ln_g.reshape(1, d), ln_b.reshape(1, d))


def _pool_mid_kernel(p_ref, halo_ref, wg_ref, sc_ref, o_ref, ext_ref, *, tt, n_prompt_tiles, tiles_per_seq):
    n = pl.program_id(0)
    d = p_ref.shape[-1]
    gw = d // len(POOL_WINDOWS)
    ext_ref[0:POOL_HALO, :] = halo_ref[...]
    ext_ref[POOL_HALO:POOL_HALO + tt, :] = p_ref[...]
    pos0 = jnp.where(n < n_prompt_tiles, (n % tiles_per_seq) * tt, PAST_LEN)
    pos = pos0 + lax.broadcasted_iota(jnp.int32, (tt, 1), 0)
    for gi, w in enumerate(POOL_WINDOWS):
        c0 = gi * gw
        s = ext_ref[POOL_HALO:POOL_HALO + tt, c0:c0 + gw]
        for back in range(1, w):
            s = s + ext_ref[POOL_HALO - back:POOL_HALO - back + tt, c0:c0 + gw]
        cnt = jnp.minimum(pos + 1, w).astype(F32)
        pooled = s / cnt - p_ref[:, c0:c0 + gw]
        z = _dot(pooled.astype(BF16), wg_ref[gi]) * sc_ref[:, c0:c0 + gw]
        o_ref[:, c0:c0 + gw] = z.astype(o_ref.dtype)


def _pool_mid(p_tiles, halo, w_grp, scale, n_prompt_tiles, tiles_per_seq):
    n, tt, d = p_tiles.shape
    blocks = (_nbytes((tt, d), F32) + _nbytes((POOL_HALO, d), F32) + _nbytes(w_grp.shape, BF16)
              + _nbytes((SUBLANES, d), F32) + _nbytes((tt, d), BF16))
    return pl.pallas_call(
        functools.partial(_pool_mid_kernel, tt=tt, n_prompt_tiles=n_prompt_tiles, tiles_per_seq=tiles_per_seq),
        grid=(n,),
        in_specs=[
            pl.BlockSpec((None, tt, d), lambda i: (i, 0, 0)),
            pl.BlockSpec((None, POOL_HALO, d), lambda i: (i, 0, 0)),
            pl.BlockSpec(w_grp.shape, lambda i: (0, 0, 0)),
            pl.BlockSpec((1, d), lambda i: (0, 0)),
        ],
        out_specs=pl.BlockSpec((None, tt, d), lambda i: (i, 0, 0)),
        out_shape=jax.ShapeDtypeStruct((n, tt, d), BF16),
        scratch_shapes=[pltpu.VMEM((POOL_HALO + tt, d), F32)],
        compiler_params=pltpu.CompilerParams(
            dimension_semantics=("parallel",),
            vmem_limit_bytes=_vmem_limit(blocks, _nbytes((POOL_HALO + tt, d), F32), 4 * _nbytes((tt, d), F32))),
        name="pool_mid",
    )(p_tiles, halo, w_grp, scale.reshape(1, d))


def _attn_kernel(q_ref, k0_ref, k1_ref, k2_ref, v0_ref, v1_ref, v2_ref, bias_ref, sink_ref, o_ref):
    keys = SWA_BAND + CHUNK
    pair_heads = Q_PER_KV // 2
    low_half = lax.broadcasted_iota(jnp.int32, (keys, LANES), 1) < HEAD_DIM
    tn_dims = (((1,), (1,)), ((), ()))
    nt_dims = (((0,), (0,)), ((), ()))

    units = []
    for m in range(N_KV_HEADS // 2):
        cols = slice(m * LANES, (m + 1) * LANES)
        kb = jnp.concatenate([k0_ref[:, cols], k1_ref[:, cols], k2_ref[:, cols]], axis=0) * ATTN_SCALE
        vb = jnp.concatenate([v0_ref[:, cols], v1_ref[:, cols], v2_ref[:, cols]], axis=0)
        kswap = pltpu.roll(kb, HEAD_DIM, axis=1)
        vswap = pltpu.roll(vb, HEAD_DIM, axis=1)
        for sub in range(2):
            kh = 2 * m + sub
            k_low, k_high = (kb, kswap) if sub == 0 else (kswap, kb)
            v_low, v_high = (vb, vswap) if sub == 0 else (vswap, vb)
            units.append((kh, 0, jnp.where(low_half, k_low, 0.0).astype(BF16),
                          jnp.where(low_half, v_low, 0.0).astype(BF16)))
            units.append((kh, 1, jnp.where(low_half, 0.0, k_high).astype(BF16),
                          jnp.where(low_half, 0.0, v_high).astype(BF16)))

    qstacks = []
    for kh in range(N_KV_HEADS):
        qstacks.append(jnp.concatenate(
            [q_ref[:, (kh * pair_heads + j) * LANES:(kh * pair_heads + j + 1) * LANES] for j in range(pair_heads)],
            axis=0).astype(BF16))

    scores = []
    for kh, par, kpad, _ in units:
        s = lax.dot_general(kpad, qstacks[kh], tn_dims, preferred_element_type=F32)
        scores.append(s + bias_ref[2 * kh + par])

    probs = []
    for (kh, par, _, _), s in zip(units, scores):
        sk = sink_ref[2 * kh + par]
        mx = jnp.maximum(jnp.max(s, axis=0, keepdims=True), sk)
        e = jnp.exp(s - mx)
        denom = jnp.sum(e, axis=0, keepdims=True) + jnp.exp(sk - mx)
        probs.append((e * (1.0 / denom)).astype(BF16))

    for kh in range(N_KV_HEADS):
        acc = None
        for (ukh, par, _, vpad), p in zip(units, probs):
            if ukh != kh:
                continue
            o = lax.dot_general(p, vpad, nt_dims, preferred_element_type=F32)
            acc = o if acc is None else acc + o
        for j in range(pair_heads):
            c0 = (kh * pair_heads + j) * LANES
            o_ref[:, c0:c0 + LANES] = acc[j * CHUNK:(j + 1) * CHUNK, :].astype(o_ref.dtype)


def _attention(qkv, kpad, vpad, bias, sinks, lay):
    t = qkv.shape[0]
    hw = N_HEADS * HEAD_DIM
    kvw = N_KV_HEADS * HEAD_DIM
    n_hist = SWA_BAND // CHUNK
    cps = lay.sp // CHUNK
    n_prompt_chunks = lay.bp * cps

    def kv_map(off):
        def index(c):
            base = jnp.where(c < n_prompt_chunks,
                             (c // cps) * (cps + n_hist) + c % cps,
                             lay.bp * (cps + n_hist) + (c - n_prompt_chunks) * (n_hist + 1))
            return (base + off, 0)
        return index

    def bias_map(c):
        return (jnp.where(c < n_prompt_chunks, jnp.minimum(c % cps, n_hist), n_hist), 0, 0, 0)

    kv_specs = [pl.BlockSpec((CHUNK, kvw), kv_map(o)) for o in range(n_hist + 1)]
    blocks = (_nbytes((CHUNK, hw), F32) + 6 * _nbytes((CHUNK, kvw), F32) + _nbytes(bias.shape[1:], F32)
              + _nbytes((2 * N_KV_HEADS, SUBLANES, bias.shape[-1]), F32) + _nbytes((CHUNK, hw), BF16))
    return pl.pallas_call(
        _attn_kernel,
        grid=(t // CHUNK,),
        in_specs=[pl.BlockSpec((CHUNK, hw), lambda c: (c, 0))] + kv_specs + kv_specs + [
            pl.BlockSpec((None,) + bias.shape[1:], bias_map),
            pl.BlockSpec(sinks.shape, lambda c: (0, 0, 0)),
        ],
        out_specs=pl.BlockSpec((CHUNK, hw), lambda c: (c, 0)),
        out_shape=jax.ShapeDtypeStruct((t, hw), BF16),
        compiler_params=pltpu.CompilerParams(
            dimension_semantics=("parallel",),
            vmem_limit_bytes=_vmem_limit(blocks, 0, 6 * _nbytes(bias.shape[1:], F32))),
        name="swa_attention",
    )(qkv, kpad, kpad, kpad, vpad, vpad, vpad, bias, sinks)


def _alibi_tables(swa_sinks):
    n_hist = SWA_BAND // CHUNK
    keys = SWA_BAND + CHUNK
    pair_heads = Q_PER_KV // 2

    def by_unit(per_head):
        return per_head.reshape(N_KV_HEADS, pair_heads, 2).transpose(0, 2, 1).reshape(2 * N_KV_HEADS, pair_heads)

    slopes = by_unit(jnp.exp2(-8.0 * jnp.arange(1, N_HEADS + 1, dtype=F32) / N_HEADS))
    dist = jnp.abs(jnp.arange(CHUNK)[None, :] + SWA_BAND - jnp.arange(keys)[:, None]).astype(F32)
    bias = (-slopes[:, None, :, None] * dist[None, :, None, :]).reshape(2 * N_KV_HEADS, keys, pair_heads * CHUNK)
    key_idx = jnp.arange(keys)[None, :, None]
    bias = jnp.stack([jnp.where(key_idx >= (n_hist - v) * CHUNK, bias, -jnp.inf) for v in range(n_hist + 1)])
    sinks = jnp.broadcast_to(by_unit(swa_sinks.astype(F32))[:, None, :, None],
                             (2 * N_KV_HEADS, 1, pair_heads, CHUNK)).reshape(2 * N_KV_HEADS, 1, pair_heads * CHUNK)
    return bias, sinks


def _gmlp_mid_kernel(u_ref, v_ref, lng_ref, lnb_ref, ws_ref, bs_ref, o_ref, vn_ref, *, n_prompt_blocks):
    n = pl.program_id(0)
    is_sample = n >= n_prompt_blocks
    v = v_ref[...]
    vc = v - jnp.mean(v, axis=-1, keepdims=True)
    var = jnp.mean(vc * vc, axis=-1, keepdims=True)
    vn = vc * lax.rsqrt(var + EPS) * lng_ref[...] + lnb_ref[...]
    vn_ref[...] = vn
    vnb = vn.astype(BF16)
    ll = GMLP_CHUNK
    rb = lax.broadcasted_iota(jnp.int32, (ll, ll), 0) // CHUNK
    cb = lax.broadcasted_iota(jnp.int32, (ll, ll), 1) // CHUNK
    back = rb - cb
    mask = (back >= 0) & (back <= jnp.where(is_sample, 0, 1))
    gw = u_ref.shape[-1] // GMLP_GROUPS
    for g in range(GMLP_GROUPS):
        w = jnp.where(mask, ws_ref[g], 0.0).astype(BF16)
        mixed = _dot(w, vnb[:, g * gw:(g + 1) * gw]) + bs_ref[g]
        o_ref[:, g * gw:(g + 1) * gw] = (u_ref[:, g * gw:(g + 1) * gw].astype(F32) * mixed).astype(o_ref.dtype)


def _gmlp_mid(u, v, ln_g, ln_b, ws2, bs2, n_prompt_blocks):
    t, gwid = u.shape
    ll = GMLP_CHUNK
    sel = lambda n: jnp.where(n >= n_prompt_blocks, 1, 0)
    blocks = (_nbytes((ll, gwid), BF16) + 2 * _nbytes((ll, gwid), F32) + 2 * _nbytes((SUBLANES, gwid), F32)
              + _nbytes((GMLP_GROUPS, ll, ll), F32) + _nbytes((GMLP_GROUPS, ll, LANES), F32)
              + _nbytes((ll, gwid), BF16))
    return pl.pallas_call(
        functools.partial(_gmlp_mid_kernel, n_prompt_blocks=n_prompt_blocks),
        grid=(t // ll,),
        in_specs=[
            pl.BlockSpec((ll, gwid), lambda n: (n, 0)),
            pl.BlockSpec((ll, gwid), lambda n: (n, 0)),
            pl.BlockSpec((1, gwid), lambda n: (0, 0)),
            pl.BlockSpec((1, gwid), lambda n: (0, 0)),
            pl.BlockSpec((None, GMLP_GROUPS, ll, ll), lambda n: (sel(n), 0, 0, 0)),
            pl.BlockSpec((None, GMLP_GROUPS, ll, 1), lambda n: (sel(n), 0, 0, 0)),
        ],
        out_specs=(pl.BlockSpec((ll, gwid), lambda n: (n, 0)), pl.BlockSpec((ll, gwid), lambda n: (n, 0))),
        out_shape=(jax.ShapeDtypeStruct((t, gwid), BF16), jax.ShapeDtypeStruct((t, gwid), F32)),
        compiler_params=pltpu.CompilerParams(
            dimension_semantics=("parallel",),
            vmem_limit_bytes=_vmem_limit(blocks, 0, 6 * _nbytes((ll, gwid), F32))),
        name="gmlp_mid",
    )(u, v, ln_g.reshape(1, gwid), ln_b.reshape(1, gwid), ws2, bs2)


def _final_norm_kernel(x_ref, g_ref, o_ref):
    x = x_ref[...]
    ms = jnp.mean(x * x, axis=-1, keepdims=True)
    o_ref[...] = x * lax.rsqrt(ms + EPS) * g_ref[...]


def _final_norm(x, g):
    t, d = x.shape
    tm = _pick(t, (256, 128))
    return pl.pallas_call(
        _final_norm_kernel,
        grid=(t // tm,),
        in_specs=[pl.BlockSpec((tm, d), lambda i: (i, 0)), pl.BlockSpec((1, d), lambda i: (0, 0))],
        out_specs=pl.BlockSpec((tm, d), lambda i: (i, 0)),
        out_shape=jax.ShapeDtypeStruct((t, d), F32),
        compiler_params=pltpu.CompilerParams(
            dimension_semantics=("parallel",),
            vmem_limit_bytes=_vmem_limit(2 * _nbytes((tm, d), F32), 0, 2 * _nbytes((tm, d), F32))),
        name="final_norm",
    )(x, g.reshape(1, d))


def _halo_tiles(y, cache, halo_rows, lay):
    d = y.shape[-1]
    tps = lay.sp // CHUNK
    yp = y[:lay.tp].reshape(lay.bp, tps, CHUNK, d)
    hp = jnp.pad(yp[:, :-1, CHUNK - halo_rows:, :], ((0, 0), (1, 0), (0, 0), (0, 0)))
    hs = jnp.pad(cache, ((0, 0), (halo_rows - cache.shape[1], 0), (0, 0)))
    return jnp.concatenate([hp.reshape(lay.bp * tps, halo_rows, d), hs], axis=0)


def _state_tail(y, cache, keep, lay):
    d = y.shape[-1]
    yp = y[:lay.tp].reshape(lay.bp, lay.sp, d)
    ys = y[lay.tp:].reshape(lay.bs, lay.ts, d)
    return yp[:, lay.sp - keep:], jnp.concatenate([cache, ys], axis=1)[:, -keep:]


def kernel(x_prompt, x_sample, c_prompt, c_sample, cache_conv, cache_pool, cache_swa_k, cache_swa_v, ada_w, ada_b, norm_g, ffn_w1, ffn_w3, ffn_w2, conv_w_pw1, conv_b_pw1, conv_w_dw, conv_b_dw, conv_ln_g, conv_ln_b, conv_w_pw2, conv_b_pw2, pool_w_in, pool_w_grp, pool_scale, pool_w_out, swa_wq, swa_wk, swa_wv, swa_wo, swa_sinks, gmlp_w_in, gmlp_b_in, gmlp_ln_g, gmlp_ln_b, gmlp_w_s, gmlp_b_s, gmlp_w_out, gmlp_b_out, final_g):
    bp, sp, d = x_prompt.shape
    bs, ts, _ = x_sample.shape
    lay = Layout(bp, sp, bs, ts)
    depth = ada_w.shape[0]
    kvw = N_KV_HEADS * HEAD_DIM
    hw = N_HEADS * HEAD_DIM

    x = jnp.concatenate([x_prompt.reshape(lay.tp, d), x_sample.reshape(lay.tsamp, d)], axis=0)
    c_all = jnp.pad(jnp.concatenate([c_prompt, c_sample], axis=0), ((0, SEQ_PAD - bp - bs), (0, 0)))
    mod = _ada_table(c_all, ada_w, ada_b).reshape(depth, 3, 3, SEQ_PAD, d)
    norm_g4 = norm_g.reshape(depth, 3, 1, d)

    w1b, w3b, w2b = ffn_w1.astype(BF16), ffn_w3.astype(BF16), ffn_w2.astype(BF16)

    outs = {}
    for i in range(depth):
        x = _ffn(x, mod, norm_g4, w1b, w3b, w2b, i, 0, 0, lay)
        kind = i % 4
        if kind == 0:
            g = _inproj(x, mod, norm_g4, i, conv_w_pw1, conv_b_pw1, lay, "glu")
            halo = _halo_tiles(g, cache_conv, CONV_HALO, lay)
            a = _conv_mid(g.reshape(lay.t // CHUNK, CHUNK, d), halo, conv_w_dw, conv_b_dw, conv_ln_g, conv_ln_b)
            x = _outproj(x, a.reshape(lay.t, d), mod, i, conv_w_pw2.astype(BF16), conv_b_pw2, lay)
            outs["conv"] = _state_tail(g, cache_conv, CONV_STATE, lay)
        elif kind == 1:
            p = _inproj(x, mod, norm_g4, i, pool_w_in, None, lay, "single")
            halo = _halo_tiles(p, cache_pool, POOL_HALO, lay)
            a = _pool_mid(p.reshape(lay.t // CHUNK, CHUNK, d), halo, pool_w_grp.astype(BF16), pool_scale,
                          lay.tp // CHUNK, lay.sp // CHUNK)
            x = _outproj(x, a.reshape(lay.t, d), mod, i, pool_w_out.astype(BF16), None, lay)
            outs["pool"] = _state_tail(p, cache_pool, POOL_STATE, lay)
        elif kind == 2:
            w_qkv = jnp.concatenate([swa_wq, swa_wk, swa_wv], axis=1)
            qkv = _inproj(x, mod, norm_g4, i, w_qkv, None, lay, "single")
            r = cache_swa_k.shape[1]
            assert r == SWA_BAND

            def padded(col0, cache):
                new = qkv[:, col0:col0 + kvw]
                newp = jnp.pad(new[:lay.tp].reshape(bp, sp, kvw), ((0, 0), (SWA_BAND, 0), (0, 0)))
                news = jnp.concatenate([cache.reshape(bs, r, kvw), new[lay.tp:].reshape(bs, ts, kvw)], axis=1)
                tail_p = new[:lay.tp].reshape(bp, sp, kvw)[:, sp - SWA_BAND:]
                tail_s = news[:, ts:]
                both = jnp.concatenate([newp.reshape(-1, kvw), news.reshape(-1, kvw)], axis=0)
                return both, tail_p.reshape(bp, SWA_BAND, N_KV_HEADS, HEAD_DIM), tail_s.reshape(bs, r, N_KV_HEADS, HEAD_DIM)

            kpad, k_p, k_s = padded(hw, cache_swa_k)
            vpad, v_p, v_s = padded(hw + kvw, cache_swa_v)
            bias, sinks = _alibi_tables(swa_sinks)
            a = _attention(qkv, kpad, vpad, bias, sinks, lay)
            x = _outproj(x, a, mod, i, swa_wo.astype(BF16), None, lay)
            outs["kv"] = (k_p, v_p, k_s, v_s)
        else:
            u, v = _inproj(x, mod, norm_g4, i, gmlp_w_in, gmlp_b_in, lay, "gelu2")
            ll = GMLP_CHUNK
            ws2 = jnp.stack([gmlp_w_s[:, :ll, :ll], jnp.tile(gmlp_w_s[:, :ts, :ts], (1, ll // ts, ll // ts))])
            bs2 = jnp.stack([gmlp_b_s[:, :ll], jnp.tile(gmlp_b_s[:, :ts], (1, ll // ts))])[..., None]
            a, vn = _gmlp_mid(u, v, gmlp_ln_g, gmlp_ln_b, ws2, bs2, lay.tp // ll)
            x = _outproj(x, a, mod, i, gmlp_w_out.astype(BF16), gmlp_b_out, lay)
            outs["gmlp_v"] = vn[lay.tp:].reshape(bs, ts, -1)
        x = _ffn(x, mod, norm_g4, w1b, w3b, w2b, i, 2, 1, lay)

    y = _final_norm(x, final_g)
    y_prompt = y[:lay.tp].reshape(bp, sp, d)
    y_sample = y[lay.tp:].reshape(bs, ts, d)
    conv_p, conv_s = outs["conv"]
    pool_p, pool_s = outs["pool"]
    k_p, v_p, k_s, v_s = outs["kv"]
    return (y_prompt, y_sample, conv_p, conv_s, pool_p, pool_s, k_p, v_p, k_s, v_s, outs["gmlp_v"])
```

```python
import functools

import jax
import jax.numpy as jnp
from jax import lax
from jax.experimental import pallas as pl
from jax.experimental.pallas import tpu as pltpu

F32 = jnp.float32
BF16 = jnp.bfloat16

EPS = 1e-6
MACARON = 0.5
N_MOD = 9
CONV_WIDTH = 31
CONV_STATE = CONV_WIDTH - 1
POOL_WINDOWS = (2, 4, 8, 16)
POOL_STATE = max(POOL_WINDOWS) - 1
HEAD_DIM = 64
N_KV_HEADS = 4
Q_PER_KV = 8
N_HEADS = N_KV_HEADS * Q_PER_KV
CHUNK = 64
SWA_BAND = 128
ATTN_SCALE = HEAD_DIM ** -0.5
GMLP_CHUNK = 128
GMLP_GROUPS = 8
PAST_LEN = 1024

LANES = 128
SUBLANES = 8
ROW_CHUNK = 16
CONV_HALO = 32
POOL_HALO = 16
SEQ_PAD = 32

V7X_VMEM_LIMIT_CAP = 60 * 1024 * 1024


class Layout:
    def __init__(self, bp, sp, bs, ts):
        self.bp, self.sp, self.bs, self.ts = bp, sp, bs, ts
        self.tp = bp * sp
        self.tsamp = bs * ts
        self.t = self.tp + self.tsamp
        assert sp % GMLP_CHUNK == 0 and ts == CHUNK and self.tsamp % GMLP_CHUNK == 0
        assert bp + bs <= SEQ_PAD

    def seq_of(self, row0):
        return jnp.where(row0 < self.tp, row0 // self.sp, self.bp + (row0 - self.tp) // self.ts)


def _pick(total, candidates):
    for c in candidates:
        if total % c == 0:
            return c
    raise ValueError(f"no tile for {total} in {candidates}")


def _vmem_limit(pipelined_bytes, scratch_bytes=0, temp_bytes=0):
    est = 2 * pipelined_bytes + scratch_bytes + temp_bytes + (4 << 20)
    return int(min(max(est, 16 << 20), V7X_VMEM_LIMIT_CAP))


def _nbytes(shape, dtype):
    n = 1
    for s in shape:
        n *= s
    return n * jnp.dtype(dtype).itemsize


def _dot(a, b):
    return jnp.dot(a, b, preferred_element_type=F32)


def _norm_mod_rows(h_ref, x_ref, mod_ref, ng_ref, row_base, tm, lay):
    ng = ng_ref[...]

    def body(c, carry):
        r = pl.multiple_of(c * CHUNK, CHUNK)
        seq = lay.seq_of(row_base + r)
        shift = mod_ref[0, pl.ds(seq, 1), :]
        gain = ng * (1.0 + mod_ref[1, pl.ds(seq, 1), :])
        for s in range(CHUNK // ROW_CHUNK):
            rows = pl.ds(r + s * ROW_CHUNK, ROW_CHUNK)
            xg = x_ref[rows, :]
            ms = jnp.mean(xg * xg, axis=-1, keepdims=True)
            h_ref[rows, :] = (xg * lax.rsqrt(ms + EPS) * gain + shift).astype(h_ref.dtype)
        return carry

    lax.fori_loop(0, tm // CHUNK, body, 0)


def _gated_residual_rows(o_ref, x_ref, mod_ref, b_ref, row_base, tm, lay, factor):
    def body(c, carry):
        r = pl.multiple_of(c * CHUNK, CHUNK)
        gate = factor * mod_ref[2, pl.ds(lay.seq_of(row_base + r), 1), :]
        for s in range(CHUNK // ROW_CHUNK):
            rows = pl.ds(r + s * ROW_CHUNK, ROW_CHUNK)
            acc = o_ref[rows, :]
            if b_ref is not None:
                acc = acc + b_ref[...]
            o_ref[rows, :] = x_ref[rows, :] + gate * acc
        return carry

    lax.fori_loop(0, tm // CHUNK, body, 0)


def _mod_spec(mod, layer, sub, grid_rank):
    zeros = (0,) * 3
    block = (None, None) + mod.shape[2:]
    if grid_rank == 1:
        return pl.BlockSpec(block, lambda i: (layer, sub) + zeros)
    return pl.BlockSpec(block, lambda i, j: (layer, sub) + zeros)


def _norm_g_spec(norm_g4, layer, sub):
    return pl.BlockSpec((None, None, 1, norm_g4.shape[-1]), lambda i, j: (layer, sub, 0, 0))


def _ada_kernel(c_ref, w_ref, b_ref, o_ref):
    c = c_ref[...]
    cond = (c * jax.nn.sigmoid(c)).astype(BF16)
    o_ref[...] = _dot(cond, w_ref[...].astype(BF16)) + b_ref[...]


def _ada_table(c_all, ada_w, ada_b):
    depth, d, n = ada_w.shape
    n_mod = n // d
    blocks = 2 * _nbytes((SEQ_PAD, d), F32) + _nbytes((d, d), F32) + _nbytes((SUBLANES, d), F32)
    return pl.pallas_call(
        _ada_kernel,
        grid=(depth, n_mod),
        in_specs=[
            pl.BlockSpec((SEQ_PAD, d), lambda l, j: (0, 0)),
            pl.BlockSpec((None, d, d), lambda l, j: (l, 0, j)),
            pl.BlockSpec((None, None, 1, d), lambda l, j: (l, j, 0, 0)),
        ],
        out_specs=pl.BlockSpec((None, None, SEQ_PAD, d), lambda l, j: (l, j, 0, 0)),
        out_shape=jax.ShapeDtypeStruct((depth, n_mod, SEQ_PAD, d), F32),
        compiler_params=pltpu.CompilerParams(
            dimension_semantics=("parallel", "arbitrary"),
            vmem_limit_bytes=_vmem_limit(blocks, temp_bytes=_nbytes((d, d), BF16))),
        name="ada_table",
    )(c_all, ada_w, ada_b.reshape(depth, n_mod, 1, d))


def _ffn_kernel(x_ref, mod_ref, ng_ref, w1_ref, w3_ref, w2_ref, o_ref, h_ref, *, tm, nj, lay):
    i = pl.program_id(0)
    j = pl.program_id(1)

    @pl.when(j == 0)
    def _():
        _norm_mod_rows(h_ref, x_ref, mod_ref, ng_ref, i * tm, tm, lay)
        o_ref[...] = jnp.zeros_like(o_ref)

    h = h_ref[...]
    a = _dot(h, w1_ref[...].astype(BF16))
    b = _dot(h, w3_ref[...].astype(BF16))
    g = (a * jax.nn.sigmoid(a) * b).astype(BF16)
    o_ref[...] += _dot(g, w2_ref[...].astype(BF16))

    @pl.when(j == nj - 1)
    def _():
        _gated_residual_rows(o_ref, x_ref, mod_ref, None, i * tm, tm, lay, MACARON)


def _ffn(x, mod, norm_g4, w1, w3, w2, layer, sub, which, lay):
    t, d = x.shape
    f = w1.shape[-1]
    tm = _pick(t, (1024, 768, 512, 384, 256, 128))
    tf = _pick(f, (256, 128))
    nj = f // tf
    blocks = (2 * _nbytes((tm, d), F32) + _nbytes(mod.shape[2:], F32) + 3 * _nbytes((d, tf), w1.dtype))
    temps = 2 * _nbytes((tm, tf), F32) + _nbytes((tm, tf), BF16) + 3 * _nbytes((d, tf), BF16)
    return pl.pallas_call(
        functools.partial(_ffn_kernel, tm=tm, nj=nj, lay=lay),
        grid=(t // tm, nj),
        in_specs=[
            pl.BlockSpec((tm, d), lambda i, j: (i, 0)),
            _mod_spec(mod, layer, sub, 2),
            _norm_g_spec(norm_g4, layer, sub),
            pl.BlockSpec((None, None, d, tf), lambda i, j: (layer, which, 0, j)),
            pl.BlockSpec((None, None, d, tf), lambda i, j: (layer, which, 0, j)),
            pl.BlockSpec((None, None, tf, d), lambda i, j: (layer, which, j, 0)),
        ],
        out_specs=pl.BlockSpec((tm, d), lambda i, j: (i, 0)),
        out_shape=jax.ShapeDtypeStruct((t, d), F32),
        scratch_shapes=[pltpu.VMEM((tm, d), BF16)],
        compiler_params=pltpu.CompilerParams(
            dimension_semantics=("parallel", "arbitrary"),
            vmem_limit_bytes=_vmem_limit(blocks, _nbytes((tm, d), BF16), temps)),
        name="ffn",
    )(x, mod, norm_g4, w1, w3, w2)


def _inproj_kernel(x_ref, mod_ref, ng_ref, *refs, mode, has_bias, tm, lay):
    i = pl.program_id(0)
    j = pl.program_id(1)
    h_ref = refs[-1]

    @pl.when(j == 0)
    def _():
        _norm_mod_rows(h_ref, x_ref, mod_ref, ng_ref, i * tm, tm, lay)

    h = h_ref[...]
    if mode == "single":
        if has_bias:
            w_ref, b_ref, o_ref = refs[:3]
            o_ref[...] = _dot(h, w_ref[...].astype(BF16)) + b_ref[...]
        else:
            w_ref, o_ref = refs[:2]
            o_ref[...] = _dot(h, w_ref[...].astype(BF16))
    elif mode == "glu":
        wa_ref, wb_ref, ba_ref, bb_ref, o_ref = refs[:5]
        a = _dot(h, wa_ref[...].astype(BF16)) + ba_ref[...]
        b = _dot(h, wb_ref[...].astype(BF16)) + bb_ref[...]
        o_ref[...] = a * jax.nn.sigmoid(b)
    else:
        wa_ref, wb_ref, ba_ref, bb_ref, ou_ref, ov_ref = refs[:6]
        ou_ref[...] = jax.nn.gelu(_dot(h, wa_ref[...].astype(BF16)) + ba_ref[...],
                                  approximate=True).astype(ou_ref.dtype)
        ov_ref[...] = jax.nn.gelu(_dot(h, wb_ref[...].astype(BF16)) + bb_ref[...], approximate=True)


def _inproj(x, mod, norm_g4, layer, w, b, lay, mode):
    t, d = x.shape
    n = w.shape[1]
    tm = _pick(t, (1024, 768, 512, 384, 256, 128))
    tn = _pick(n if mode == "single" else n // 2, (512, 256, 128))
    x_spec = pl.BlockSpec((tm, d), lambda i, j: (i, 0))
    mod_spec = _mod_spec(mod, layer, 1, 2)
    ng_spec = _norm_g_spec(norm_g4, layer, 1)
    lo = pl.BlockSpec((tm, tn), lambda i, j: (i, j))
    if mode == "single":
        nj = n // tn
        ins = [x, mod, norm_g4, w]
        specs = [x_spec, mod_spec, ng_spec, pl.BlockSpec((d, tn), lambda i, j: (0, j))]
        if b is not None:
            ins.append(b.reshape(1, n))
            specs.append(pl.BlockSpec((1, tn), lambda i, j: (0, j)))
        out_shape = jax.ShapeDtypeStruct((t, n), F32)
        out_specs = lo
        n_w = 1
    else:
        half = n // 2
        nj = half // tn
        ins = [x, mod, norm_g4, w, w, b.reshape(1, n), b.reshape(1, n)]
        specs = [x_spec, mod_spec, ng_spec,
                 pl.BlockSpec((d, tn), lambda i, j: (0, j)),
                 pl.BlockSpec((d, tn), lambda i, j: (0, j + nj)),
                 pl.BlockSpec((1, tn), lambda i, j: (0, j)),
                 pl.BlockSpec((1, tn), lambda i, j: (0, j + nj))]
        n_w = 2
        if mode == "glu":
            out_shape = jax.ShapeDtypeStruct((t, half), F32)
            out_specs = lo
        else:
            out_shape = (jax.ShapeDtypeStruct((t, half), BF16), jax.ShapeDtypeStruct((t, half), F32))
            out_specs = (lo, lo)
    blocks = (_nbytes((tm, d), F32) + _nbytes(mod.shape[2:], F32) + n_w * _nbytes((d, tn), w.dtype)
              + n_w * _nbytes((tm, tn), F32))
    return pl.pallas_call(
        functools.partial(_inproj_kernel, mode=mode, has_bias=b is not None, tm=tm, lay=lay),
        grid=(t // tm, nj),
        in_specs=specs,
        out_specs=out_specs,
        out_shape=out_shape,
        scratch_shapes=[pltpu.VMEM((tm, d), BF16)],
        compiler_params=pltpu.CompilerParams(
            dimension_semantics=("parallel", "arbitrary"),
            vmem_limit_bytes=_vmem_limit(blocks, _nbytes((tm, d), BF16),
                                         n_w * (3 * _nbytes((tm, tn), F32) + _nbytes((d, tn), BF16)))),
        name="inproj_" + mode,
    )(*ins)


def _outproj_kernel(x_ref, a_ref, mod_ref, w_ref, *refs, has_bias, tm, nk, lay):
    i = pl.program_id(0)
    k = pl.program_id(1)
    b_ref, o_ref = (refs[0], refs[1]) if has_bias else (None, refs[0])

    @pl.when(k == 0)
    def _():
        o_ref[...] = jnp.zeros_like(o_ref)

    o_ref[...] += _dot(a_ref[...], w_ref[...])

    @pl.when(k == nk - 1)
    def _():
        _gated_residual_rows(o_ref, x_ref, mod_ref, b_ref, i * tm, tm, lay, 1.0)


def _outproj(x, a, mod, layer, w, b, lay):
    t, d = x.shape
    kdim = a.shape[1]
    tm = _pick(t, (512, 384, 256, 128))
    tk = _pick(kdim, (2048, 1024, 512))
    nk = kdim // tk
    ins = [x, a, mod, w]
    specs = [pl.BlockSpec((tm, d), lambda i, k: (i, 0)),
             pl.BlockSpec((tm, tk), lambda i, k: (i, k)),
             _mod_spec(mod, layer, 1, 2),
             pl.BlockSpec((tk, d), lambda i, k: (k, 0))]
    if b is not None:
        ins.append(b.reshape(1, d))
        specs.append(pl.BlockSpec((1, d), lambda i, k: (0, 0)))
    blocks = (2 * _nbytes((tm, d), F32) + _nbytes((tm, tk), BF16) + _nbytes(mod.shape[2:], F32)
              + _nbytes((tk, d), BF16))
    return pl.pallas_call(
        functools.partial(_outproj_kernel, has_bias=b is not None, tm=tm, nk=nk, lay=lay),
        grid=(t // tm, nk),
        in_specs=specs,
        out_specs=pl.BlockSpec((tm, d), lambda i, k: (i, 0)),
        out_shape=jax.ShapeDtypeStruct((t, d), F32),
        compiler_params=pltpu.CompilerParams(
            dimension_semantics=("parallel", "arbitrary"),
            vmem_limit_bytes=_vmem_limit(blocks)),
        name="outproj",
    )(*ins)


def _fill_history(ext_ref, prev_ref, cache_ref, n, n_prompt_tiles, tiles_per_seq):
    rows = prev_ref.shape[0]
    is_sample = n >= n_prompt_tiles
    starts_seq = jnp.logical_and(n % tiles_per_seq == 0, jnp.logical_not(is_sample))

    @pl.when(is_sample)
    def _():
        ext_ref[0:rows, :] = cache_ref[...]

    @pl.when(starts_seq)
    def _():
        ext_ref[0:rows, :] = jnp.zeros(prev_ref.shape, ext_ref.dtype)

    @pl.when(jnp.logical_not(jnp.logical_or(is_sample, starts_seq)))
    def _():
        ext_ref[0:rows, :] = prev_ref[...]


def _history_specs(rows, d, n_prompt_tiles, sub=1, s=0):
    per_tile = CHUNK // rows
    return [pl.BlockSpec((None, rows, d), lambda i: (jnp.maximum((i * sub + s) * per_tile - 1, 0), 0, 0)),
            pl.BlockSpec((None, rows, d), lambda i: (jnp.maximum(i * sub + s - n_prompt_tiles, 0), 0, 0))]


def _pad_cache(cache, rows):
    return jnp.pad(cache, ((0, 0), (rows - cache.shape[1], 0), (0, 0)))


def _conv_mid_kernel(g_ref, prev_ref, cache_ref, wdw_ref, bdw_ref, lng_ref, lnb_ref, o_ref, ext_ref, y_ref, *,
                     tt, cb, n_prompt_tiles, tiles_per_seq):
    d = g_ref.shape[-1]
    _fill_history(ext_ref, prev_ref, cache_ref, pl.program_id(0), n_prompt_tiles, tiles_per_seq)
    ext_ref[CONV_HALO:CONV_HALO + tt, :] = g_ref[...]
    ext_ref[CONV_HALO + tt:CONV_HALO + tt + SUBLANES, :] = jnp.zeros((SUBLANES, d), F32)
    first = CONV_HALO - CONV_STATE
    for c0 in range(0, d, cb):
        cols = slice(c0, c0 + cb)
        y = None
        for r in range(SUBLANES):
            u = None
            for k in range(CONV_WIDTH):
                if (first + k) % SUBLANES != r:
                    continue
                a0 = (first + k) - r
                term = ext_ref[a0:a0 + tt + SUBLANES, cols] * wdw_ref[k:k + 1, cols]
                u = term if u is None else u + term
            if u is None:
                continue
            piece = u[r:r + tt, :]
            y = piece if y is None else y + piece
        y_ref[:, cols] = y + bdw_ref[:, cols]
    y = y_ref[...]
    yc = y - jnp.mean(y, axis=-1, keepdims=True)
    var = jnp.mean(yc * yc, axis=-1, keepdims=True)
    z = yc * lax.rsqrt(var + EPS) * lng_ref[...] + lnb_ref[...]
    o_ref[...] = (z * jax.nn.sigmoid(z)).astype(o_ref.dtype)


def _conv_mid(g, cache, w_dw, b_dw, ln_g, ln_b, lay):
    t, d = g.shape
    tt = CHUNK
    n = t // tt
    n_prompt_tiles = lay.tp // tt
    cb = 2 * LANES
    blocks = (_nbytes((tt, d), F32) + 2 * _nbytes((CONV_HALO, d), F32) + _nbytes((32, d), F32)
              + 3 * _nbytes((SUBLANES, d), F32) + _nbytes((tt, d), BF16))
    vec = pl.BlockSpec((1, d), lambda i: (0, 0))
    ext_rows = CONV_HALO + tt + SUBLANES
    return pl.pallas_call(
        functools.partial(_conv_mid_kernel, tt=tt, cb=cb, n_prompt_tiles=n_prompt_tiles,
                          tiles_per_seq=lay.sp // tt),
        grid=(n,),
        in_specs=[pl.BlockSpec((None, tt, d), lambda i: (i, 0, 0))]
        + _history_specs(CONV_HALO, d, n_prompt_tiles) + [
            pl.BlockSpec((CONV_WIDTH, d), lambda i: (0, 0)),
            vec, vec, vec,
        ],
        out_specs=pl.BlockSpec((None, tt, d), lambda i: (i, 0, 0)),
        out_shape=jax.ShapeDtypeStruct((n, tt, d), BF16),
        scratch_shapes=[pltpu.VMEM((ext_rows, d), F32), pltpu.VMEM((tt, d), F32)],
        compiler_params=pltpu.CompilerParams(
            dimension_semantics=("parallel",),
            vmem_limit_bytes=_vmem_limit(blocks, _nbytes((ext_rows + tt, d), F32), 4 * _nbytes((tt, d), F32))),
        name="conv_mid",
    )(g.reshape(n, tt, d), g.reshape(t // CONV_HALO, CONV_HALO, d), _pad_cache(cache, CONV_HALO),
      w_dw, b_dw.reshape(1, d), ln_g.reshape(1, d), ln_b.reshape(1, d))


def _pool_mid_kernel(p_ref, *refs, sub, n_prompt_tiles, tiles_per_seq):
    hist_refs = refs[:2 * sub]
    wg_ref, sc_ref, o_ref, ext_ref, pooled_ref = refs[2 * sub:]
    n = pl.program_id(0)
    d = p_ref.shape[-1]
    gw = d // len(POOL_WINDOWS)
    for s in range(sub):
        tile = n * sub + s
        rows = slice(s * CHUNK, (s + 1) * CHUNK)
        _fill_history(ext_ref, hist_refs[2 * s], hist_refs[2 * s + 1], tile, n_prompt_tiles, tiles_per_seq)
        ext_ref[POOL_HALO:POOL_HALO + CHUNK, :] = p_ref[rows, :]
        pos0 = jnp.where(tile < n_prompt_tiles, (tile % tiles_per_seq) * CHUNK, PAST_LEN)
        pos = pos0 + lax.broadcasted_iota(jnp.int32, (CHUNK, 1), 0)
        for gi, w in enumerate(POOL_WINDOWS):
            cols = slice(gi * gw, (gi + 1) * gw)
            acc = ext_ref[POOL_HALO:POOL_HALO + CHUNK, cols]
            for back in range(1, w):
                acc = acc + ext_ref[POOL_HALO - back:POOL_HALO - back + CHUNK, cols]
            inv_cnt = 1.0 / jnp.minimum(pos + 1, w).astype(F32)
            pooled_ref[rows, cols] = (acc * inv_cnt - p_ref[rows, cols]).astype(pooled_ref.dtype)
    for gi in range(len(POOL_WINDOWS)):
        cols = slice(gi * gw, (gi + 1) * gw)
        z = _dot(pooled_ref[:, cols], wg_ref[gi]) * sc_ref[:, cols]
        o_ref[:, cols] = z.astype(o_ref.dtype)


def _pool_mid(p, cache, w_grp, scale, lay):
    t, d = p.shape
    sub = 4
    tt = sub * CHUNK
    assert lay.tp % tt == 0 and lay.tsamp % tt == 0
    n_prompt_tiles = lay.tp // CHUNK
    hist = []
    hist_in = []
    for s in range(sub):
        hist += _history_specs(POOL_HALO, d, n_prompt_tiles, sub, s)
        hist_in += [p.reshape(t // POOL_HALO, POOL_HALO, d), _pad_cache(cache, POOL_HALO)]
    blocks = (_nbytes((tt, d), F32) + 2 * sub * _nbytes((POOL_HALO, d), F32) + _nbytes(w_grp.shape, BF16)
              + _nbytes((SUBLANES, d), F32) + _nbytes((tt, d), BF16))
    return pl.pallas_call(
        functools.partial(_pool_mid_kernel, sub=sub, n_prompt_tiles=n_prompt_tiles,
                          tiles_per_seq=lay.sp // CHUNK),
        grid=(t // tt,),
        in_specs=[pl.BlockSpec((tt, d), lambda i: (i, 0))] + hist + [
            pl.BlockSpec(w_grp.shape, lambda i: (0, 0, 0)),
            pl.BlockSpec((1, d), lambda i: (0, 0)),
        ],
        out_specs=pl.BlockSpec((tt, d), lambda i: (i, 0)),
        out_shape=jax.ShapeDtypeStruct((t, d), BF16),
        scratch_shapes=[pltpu.VMEM((POOL_HALO + CHUNK, d), F32), pltpu.VMEM((tt, d), BF16)],
        compiler_params=pltpu.CompilerParams(
            dimension_semantics=("parallel",),
            vmem_limit_bytes=_vmem_limit(blocks, _nbytes((POOL_HALO + CHUNK, d), F32) + _nbytes((tt, d), BF16),
                                         4 * _nbytes((tt, d), F32))),
        name="pool_mid",
    )(p, *hist_in, w_grp, scale.reshape(1, d))


def _attn_kernel(q_ref, k0_ref, k1_ref, k2_ref, v0_ref, v1_ref, v2_ref, bias_ref, sink_ref, o_ref):
    keys = SWA_BAND + CHUNK
    pair_heads = Q_PER_KV // 2
    low_half = lax.broadcasted_iota(jnp.int32, (keys, LANES), 1) < HEAD_DIM
    tn_dims = (((1,), (1,)), ((), ()))
    nt_dims = (((0,), (0,)), ((), ()))

    units = []
    for m in range(N_KV_HEADS // 2):
        cols = slice(m * LANES, (m + 1) * LANES)
        kb = jnp.concatenate([k0_ref[:, cols], k1_ref[:, cols], k2_ref[:, cols]], axis=0) * ATTN_SCALE
        vb = jnp.concatenate([v0_ref[:, cols], v1_ref[:, cols], v2_ref[:, cols]], axis=0)
        kswap = pltpu.roll(kb, HEAD_DIM, axis=1)
        vswap = pltpu.roll(vb, HEAD_DIM, axis=1)
        for sub in range(2):
            kh = 2 * m + sub
            k_low, k_high = (kb, kswap) if sub == 0 else (kswap, kb)
            v_low, v_high = (vb, vswap) if sub == 0 else (vswap, vb)
            units.append((kh, 0, jnp.where(low_half, k_low, 0.0).astype(BF16),
                          jnp.where(low_half, v_low, 0.0).astype(BF16)))
            units.append((kh, 1, jnp.where(low_half, 0.0, k_high).astype(BF16),
                          jnp.where(low_half, 0.0, v_high).astype(BF16)))

    qstacks = []
    for kh in range(N_KV_HEADS):
        qstacks.append(jnp.concatenate(
            [q_ref[:, (kh * pair_heads + j) * LANES:(kh * pair_heads + j + 1) * LANES] for j in range(pair_heads)],
            axis=0).astype(BF16))

    scores = []
    for kh, par, kpad, _ in units:
        s = lax.dot_general(kpad, qstacks[kh], tn_dims, preferred_element_type=F32)
        scores.append(s + bias_ref[2 * kh + par])

    probs = []
    for (kh, par, _, _), s in zip(units, scores):
        sk = sink_ref[2 * kh + par]
        mx = jnp.maximum(jnp.max(s, axis=0, keepdims=True), sk)
        e = jnp.exp(s - mx)
        denom = jnp.sum(e, axis=0, keepdims=True) + jnp.exp(sk - mx)
        probs.append((e * (1.0 / denom)).astype(BF16))

    for kh in range(N_KV_HEADS):
        acc = None
        for (ukh, par, _, vpad), p in zip(units, probs):
            if ukh != kh:
                continue
            o = lax.dot_general(p, vpad, nt_dims, preferred_element_type=F32)
            acc = o if acc is None else acc + o
        for j in range(pair_heads):
            c0 = (kh * pair_heads + j) * LANES
            o_ref[:, c0:c0 + LANES] = acc[j * CHUNK:(j + 1) * CHUNK, :].astype(o_ref.dtype)


def _attention(qkv, kpad, vpad, bias, sinks, lay):
    t = qkv.shape[0]
    hw = N_HEADS * HEAD_DIM
    kvw = N_KV_HEADS * HEAD_DIM
    n_hist = SWA_BAND // CHUNK
    cps = lay.sp // CHUNK
    n_prompt_chunks = lay.bp * cps

    def kv_map(off):
        def index(c):
            base = jnp.where(c < n_prompt_chunks,
                             (c // cps) * (cps + n_hist) + c % cps,
                             lay.bp * (cps + n_hist) + (c - n_prompt_chunks) * (n_hist + 1))
            return (base + off, 0)
        return index

    def bias_map(c):
        return (jnp.where(c < n_prompt_chunks, jnp.minimum(c % cps, n_hist), n_hist), 0, 0, 0)

    kv_specs = [pl.BlockSpec((CHUNK, kvw), kv_map(o)) for o in range(n_hist + 1)]
    blocks = (_nbytes((CHUNK, hw), F32) + 6 * _nbytes((CHUNK, kvw), F32) + _nbytes(bias.shape[1:], F32)
              + _nbytes((2 * N_KV_HEADS, SUBLANES, bias.shape[-1]), F32) + _nbytes((CHUNK, hw), BF16))
    return pl.pallas_call(
        _attn_kernel,
        grid=(t // CHUNK,),
        in_specs=[pl.BlockSpec((CHUNK, hw), lambda c: (c, 0))] + kv_specs + kv_specs + [
            pl.BlockSpec((None,) + bias.shape[1:], bias_map),
            pl.BlockSpec(sinks.shape, lambda c: (0, 0, 0)),
        ],
        out_specs=pl.BlockSpec((CHUNK, hw), lambda c: (c, 0)),
        out_shape=jax.ShapeDtypeStruct((t, hw), BF16),
        compiler_params=pltpu.CompilerParams(
            dimension_semantics=("parallel",),
            vmem_limit_bytes=_vmem_limit(blocks, 0, 6 * _nbytes(bias.shape[1:], F32))),
        name="swa_attention",
    )(qkv, kpad, kpad, kpad, vpad, vpad, vpad, bias, sinks)


def _alibi_tables(swa_sinks):
    n_hist = SWA_BAND // CHUNK
    keys = SWA_BAND + CHUNK
    pair_heads = Q_PER_KV // 2

    def by_unit(per_head):
        return per_head.reshape(N_KV_HEADS, pair_heads, 2).transpose(0, 2, 1).reshape(2 * N_KV_HEADS, pair_heads)

    slopes = by_unit(jnp.exp2(-8.0 * jnp.arange(1, N_HEADS + 1, dtype=F32) / N_HEADS))
    dist = jnp.abs(jnp.arange(CHUNK)[None, :] + SWA_BAND - jnp.arange(keys)[:, None]).astype(F32)
    bias = (-slopes[:, None, :, None] * dist[None, :, None, :]).reshape(2 * N_KV_HEADS, keys, pair_heads * CHUNK)
    key_idx = jnp.arange(keys)[None, :, None]
    bias = jnp.stack([jnp.where(key_idx >= (n_hist - v) * CHUNK, bias, -jnp.inf) for v in range(n_hist + 1)])
    sinks = jnp.broadcast_to(by_unit(swa_sinks.astype(F32))[:, None, :, None],
                             (2 * N_KV_HEADS, 1, pair_heads, CHUNK)).reshape(2 * N_KV_HEADS, 1, pair_heads * CHUNK)
    return bias, sinks


def _gmlp_mid_kernel(u_ref, v_ref, lng_ref, lnb_ref, ws_ref, bs_ref, o_ref, vn_ref, *, n_prompt_blocks):
    n = pl.program_id(0)
    is_sample = n >= n_prompt_blocks
    v = v_ref[...]
    vc = v - jnp.mean(v, axis=-1, keepdims=True)
    var = jnp.mean(vc * vc, axis=-1, keepdims=True)
    vn = vc * lax.rsqrt(var + EPS) * lng_ref[...] + lnb_ref[...]
    vn_ref[...] = vn
    vnb = vn.astype(BF16)
    ll = GMLP_CHUNK
    rb = lax.broadcasted_iota(jnp.int32, (ll, ll), 0) // CHUNK
    cb = lax.broadcasted_iota(jnp.int32, (ll, ll), 1) // CHUNK
    back = rb - cb
    mask = (back >= 0) & (back <= jnp.where(is_sample, 0, 1))
    gw = u_ref.shape[-1] // GMLP_GROUPS
    for g in range(GMLP_GROUPS):
        w = jnp.where(mask, ws_ref[g], 0.0).astype(BF16)
        mixed = _dot(w, vnb[:, g * gw:(g + 1) * gw]) + bs_ref[g]
        o_ref[:, g * gw:(g + 1) * gw] = (u_ref[:, g * gw:(g + 1) * gw].astype(F32) * mixed).astype(o_ref.dtype)


def _gmlp_mid(u, v, ln_g, ln_b, ws2, bs2, n_prompt_blocks):
    t, gwid = u.shape
    ll = GMLP_CHUNK
    sel = lambda n: jnp.where(n >= n_prompt_blocks, 1, 0)
    blocks = (_nbytes((ll, gwid), BF16) + 2 * _nbytes((ll, gwid), F32) + 2 * _nbytes((SUBLANES, gwid), F32)
              + _nbytes((GMLP_GROUPS, ll, ll), F32) + _nbytes((GMLP_GROUPS, ll, LANES), F32)
              + _nbytes((ll, gwid), BF16))
    return pl.pallas_call(
        functools.partial(_gmlp_mid_kernel, n_prompt_blocks=n_prompt_blocks),
        grid=(t // ll,),
        in_specs=[
            pl.BlockSpec((ll, gwid), lambda n: (n, 0)),
            pl.BlockSpec((ll, gwid), lambda n: (n, 0)),
            pl.BlockSpec((1, gwid), lambda n: (0, 0)),
            pl.BlockSpec((1, gwid), lambda n: (0, 0)),
            pl.BlockSpec((None, GMLP_GROUPS, ll, ll), lambda n: (sel(n), 0, 0, 0)),
            pl.BlockSpec((None, GMLP_GROUPS, ll, 1), lambda n: (sel(n), 0, 0, 0)),
        ],
        out_specs=(pl.BlockSpec((ll, gwid), lambda n: (n, 0)),
                   pl.BlockSpec((ll, gwid), lambda n: (jnp.maximum(n - n_prompt_blocks, 0), 0))),
        out_shape=(jax.ShapeDtypeStruct((t, gwid), BF16),
                   jax.ShapeDtypeStruct((t - n_prompt_blocks * ll, gwid), F32)),
        compiler_params=pltpu.CompilerParams(
            dimension_semantics=("arbitrary",),
            vmem_limit_bytes=_vmem_limit(blocks, 0, 6 * _nbytes((ll, gwid), F32))),
        name="gmlp_mid",
    )(u, v, ln_g.reshape(1, gwid), ln_b.reshape(1, gwid), ws2, bs2)


def _final_norm_kernel(x_ref, g_ref, op_ref, os_ref, *, n_prompt_tiles):
    x = x_ref[...]
    ms = jnp.mean(x * x, axis=-1, keepdims=True)
    y = x * lax.rsqrt(ms + EPS) * g_ref[...]
    os_ref[...] = y

    @pl.when(pl.program_id(0) < n_prompt_tiles)
    def _():
        op_ref[...] = y


def _final_norm(x, g, lay):
    t, d = x.shape
    tm = _pick(lay.tp, (256, 128))
    assert lay.tsamp % tm == 0
    npt = lay.tp // tm
    return pl.pallas_call(
        functools.partial(_final_norm_kernel, n_prompt_tiles=npt),
        grid=(t // tm,),
        in_specs=[pl.BlockSpec((tm, d), lambda i: (i, 0)), pl.BlockSpec((1, d), lambda i: (0, 0))],
        out_specs=(pl.BlockSpec((tm, d), lambda i: (jnp.minimum(i, npt - 1), 0)),
                   pl.BlockSpec((tm, d), lambda i: (jnp.maximum(i - npt, 0), 0))),
        out_shape=(jax.ShapeDtypeStruct((lay.tp, d), F32), jax.ShapeDtypeStruct((lay.tsamp, d), F32)),
        compiler_params=pltpu.CompilerParams(
            dimension_semantics=("arbitrary",),
            vmem_limit_bytes=_vmem_limit(3 * _nbytes((tm, d), F32), 0, 2 * _nbytes((tm, d), F32))),
        name="final_norm",
    )(x, g.reshape(1, d))


def _state_tail(y, cache, keep, lay):
    d = y.shape[-1]
    tails = jnp.stack([y[(b + 1) * lay.sp - keep:(b + 1) * lay.sp] for b in range(lay.bp)])
    ys = y[lay.tp:].reshape(lay.bs, lay.ts, d)
    return tails, jnp.concatenate([cache, ys], axis=1)[:, -keep:]


def kernel(x_prompt, x_sample, c_prompt, c_sample, cache_conv, cache_pool, cache_swa_k, cache_swa_v, ada_w, ada_b, norm_g, ffn_w1, ffn_w3, ffn_w2, conv_w_pw1, conv_b_pw1, conv_w_dw, conv_b_dw, conv_ln_g, conv_ln_b, conv_w_pw2, conv_b_pw2, pool_w_in, pool_w_grp, pool_scale, pool_w_out, swa_wq, swa_wk, swa_wv, swa_wo, swa_sinks, gmlp_w_in, gmlp_b_in, gmlp_ln_g, gmlp_ln_b, gmlp_w_s, gmlp_b_s, gmlp_w_out, gmlp_b_out, final_g):
    bp, sp, d = x_prompt.shape
    bs, ts, _ = x_sample.shape
    lay = Layout(bp, sp, bs, ts)
    depth = ada_w.shape[0]
    kvw = N_KV_HEADS * HEAD_DIM
    hw = N_HEADS * HEAD_DIM

    x = jnp.concatenate([x_prompt.reshape(lay.tp, d), x_sample.reshape(lay.tsamp, d)], axis=0)
    c_all = jnp.pad(jnp.concatenate([c_prompt, c_sample], axis=0), ((0, SEQ_PAD - bp - bs), (0, 0)))
    mod = _ada_table(c_all, ada_w, ada_b).reshape(depth, 3, 3, SEQ_PAD, d)
    norm_g4 = norm_g.reshape(depth, 3, 1, d)

    w1b, w3b, w2b = ffn_w1, ffn_w3, ffn_w2

    outs = {}
    for i in range(depth):
        x = _ffn(x, mod, norm_g4, w1b, w3b, w2b, i, 0, 0, lay)
        kind = i % 4
        if kind == 0:
            g = _inproj(x, mod, norm_g4, i, conv_w_pw1, conv_b_pw1, lay, "glu")
            a = _conv_mid(g, cache_conv, conv_w_dw, conv_b_dw, conv_ln_g, conv_ln_b, lay)
            x = _outproj(x, a.reshape(lay.t, d), mod, i, conv_w_pw2.astype(BF16), conv_b_pw2, lay)
            outs["conv"] = _state_tail(g, cache_conv, CONV_STATE, lay)
        elif kind == 1:
            p = _inproj(x, mod, norm_g4, i, pool_w_in, None, lay, "single")
            a = _pool_mid(p, cache_pool, pool_w_grp.astype(BF16), pool_scale, lay)
            x = _outproj(x, a.reshape(lay.t, d), mod, i, pool_w_out.astype(BF16), None, lay)
            outs["pool"] = _state_tail(p, cache_pool, POOL_STATE, lay)
        elif kind == 2:
            w_qkv = jnp.concatenate([swa_wq, swa_wk, swa_wv], axis=1)
            qkv = _inproj(x, mod, norm_g4, i, w_qkv, None, lay, "single")
            r = cache_swa_k.shape[1]
            assert r == SWA_BAND

            def padded(col0, cache):
                new = qkv[:, col0:col0 + kvw]
                newp = jnp.pad(new[:lay.tp].reshape(bp, sp, kvw), ((0, 0), (SWA_BAND, 0), (0, 0)))
                news = jnp.concatenate([cache.reshape(bs, r, kvw), new[lay.tp:].reshape(bs, ts, kvw)], axis=1)
                tail_p = new[:lay.tp].reshape(bp, sp, kvw)[:, sp - SWA_BAND:]
                tail_s = news[:, ts:]
                both = jnp.concatenate([newp.reshape(-1, kvw), news.reshape(-1, kvw)], axis=0)
                return both, tail_p.reshape(bp, SWA_BAND, N_KV_HEADS, HEAD_DIM), tail_s.reshape(bs, r, N_KV_HEADS, HEAD_DIM)

            kpad, k_p, k_s = padded(hw, cache_swa_k)
            vpad, v_p, v_s = padded(hw + kvw, cache_swa_v)
            bias, sinks = _alibi_tables(swa_sinks)
            a = _attention(qkv, kpad, vpad, bias, sinks, lay)
            x = _outproj(x, a, mod, i, swa_wo.astype(BF16), None, lay)
            outs["kv"] = (k_p, v_p, k_s, v_s)
        else:
            u, v = _inproj(x, mod, norm_g4, i, gmlp_w_in, gmlp_b_in, lay, "gelu2")
            ll = GMLP_CHUNK
            ws2 = jnp.stack([gmlp_w_s[:, :ll, :ll], jnp.tile(gmlp_w_s[:, :ts, :ts], (1, ll // ts, ll // ts))])
            bs2 = jnp.stack([gmlp_b_s[:, :ll], jnp.tile(gmlp_b_s[:, :ts], (1, ll // ts))])[..., None]
            a, vn = _gmlp_mid(u, v, gmlp_ln_g, gmlp_ln_b, ws2, bs2, lay.tp // ll)
            x = _outproj(x, a, mod, i, gmlp_w_out.astype(BF16), gmlp_b_out, lay)
            outs["gmlp_v"] = vn.reshape(bs, ts, -1)
        x = _ffn(x, mod, norm_g4, w1b, w3b, w2b, i, 2, 1, lay)

    y_p, y_s = _final_norm(x, final_g, lay)
    y_prompt = y_p.reshape(bp, sp, d)
    y_sample = y_s.reshape(bs, ts, d)
    conv_p, conv_s = outs["conv"]
    pool_p, pool_s = outs["pool"]
    k_p, v_p, k_s, v_s = outs["kv"]
    return (y_prompt, y_sample, conv_p, conv_s, pool_p, pool_s, k_p, v_p, k_s, v_s, outs["gmlp_v"])
```

```python
import functools

import jax
import jax.numpy as jnp
from jax import lax
from jax.experimental import pallas as pl
from jax.experimental.pallas import tpu as pltpu

F32 = jnp.float32
BF16 = jnp.bfloat16

EPS = 1e-6
MACARON = 0.5
N_MOD = 9
CONV_WIDTH = 31
CONV_STATE = CONV_WIDTH - 1
POOL_WINDOWS = (2, 4, 8, 16)
POOL_STATE = max(POOL_WINDOWS) - 1
HEAD_DIM = 64
N_KV_HEADS = 4
Q_PER_KV = 8
N_HEADS = N_KV_HEADS * Q_PER_KV
CHUNK = 64
SWA_BAND = 128
ATTN_SCALE = HEAD_DIM ** -0.5
GMLP_CHUNK = 128
GMLP_GROUPS = 8
PAST_LEN = 1024

LANES = 128
SUBLANES = 8
ROW_CHUNK = 16
CONV_HALO = 32
POOL_HALO = 16
SEQ_PAD = 32

V7X_VMEM_LIMIT_CAP = 60 * 1024 * 1024


class Layout:
    def __init__(self, bp, sp, bs, ts):
        self.bp, self.sp, self.bs, self.ts = bp, sp, bs, ts
        self.tp = bp * sp
        self.tsamp = bs * ts
        self.t = self.tp + self.tsamp
        assert sp % GMLP_CHUNK == 0 and ts == CHUNK and self.tsamp % GMLP_CHUNK == 0
        assert bp + bs <= SEQ_PAD

    def seq_of(self, row0):
        return jnp.where(row0 < self.tp, row0 // self.sp, self.bp + (row0 - self.tp) // self.ts)


def _pick(total, candidates):
    for c in candidates:
        if total % c == 0:
            return c
    raise ValueError(f"no tile for {total} in {candidates}")


def _vmem_limit(pipelined_bytes, scratch_bytes=0, temp_bytes=0):
    est = 2 * pipelined_bytes + scratch_bytes + temp_bytes + (4 << 20)
    return int(min(max(est, 16 << 20), V7X_VMEM_LIMIT_CAP))


def _nbytes(shape, dtype):
    n = 1
    for s in shape:
        n *= s
    return n * jnp.dtype(dtype).itemsize


def _dot(a, b):
    return jnp.dot(a, b, preferred_element_type=F32)


def _norm_mod_group(h_ref, h_row, x_ref, x_row, mod_ref, ng, global_row, lay):
    seq = lay.seq_of(global_row)
    shift = mod_ref[0, pl.ds(seq, 1), :]
    gain = ng * (1.0 + mod_ref[1, pl.ds(seq, 1), :])
    for s in range(CHUNK // ROW_CHUNK):
        xg = x_ref[pl.ds(x_row + s * ROW_CHUNK, ROW_CHUNK), :]
        ms = jnp.mean(xg * xg, axis=-1, keepdims=True)
        h_ref[pl.ds(h_row + s * ROW_CHUNK, ROW_CHUNK), :] = (
            xg * lax.rsqrt(ms + EPS) * gain + shift).astype(h_ref.dtype)


def _norm_mod_rows(h_ref, x_ref, mod_ref, ng_ref, row_base, tm, lay):
    ng = ng_ref[...]

    def body(c, carry):
        r = pl.multiple_of(c * CHUNK, CHUNK)
        _norm_mod_group(h_ref, r, x_ref, r, mod_ref, ng, row_base + r, lay)
        return carry

    lax.fori_loop(0, tm // CHUNK, body, 0)


class AheadPlan:
    def __init__(self, tm, n_tiles, n_steps):
        self.tm, self.n_tiles = tm, n_tiles
        groups = tm // CHUNK
        need = -(-groups // n_steps)
        self.per_step = next(g for g in range(need, groups + 1) if groups % g == 0)
        self.blocks = groups // self.per_step
        self.rows = self.per_step * CHUNK

    def spec(self, d):
        def index(i, j):
            nxt = jnp.minimum(i + 1, self.n_tiles - 1)
            return (nxt * self.blocks + jnp.minimum(j, self.blocks - 1), 0)
        return pl.BlockSpec((self.rows, d), index)

    def run(self, h_next, xa_ref, mod_ref, ng_ref, i, j, lay):
        ng = ng_ref[...]
        nxt = jnp.minimum(i + 1, self.n_tiles - 1)
        row0 = pl.multiple_of(jnp.minimum(j, self.blocks - 1) * self.rows, CHUNK)
        for g in range(self.per_step):
            _norm_mod_group(h_next, row0 + g * CHUNK, xa_ref, g * CHUNK, mod_ref, ng,
                            nxt * self.tm + row0 + g * CHUNK, lay)


def _by_parity(i, ha_ref, hb_ref, body):
    @pl.when(i % 2 == 0)
    def _():
        body(ha_ref, hb_ref)

    @pl.when(i % 2 == 1)
    def _():
        body(hb_ref, ha_ref)


def _gate_row(mod_ref, global_row, lay):
    return mod_ref[2, pl.ds(lay.seq_of(global_row), 1), :]


def _mod_spec(mod, layer, sub, grid_rank):
    zeros = (0,) * 3
    block = (None, None) + mod.shape[2:]
    if grid_rank == 1:
        return pl.BlockSpec(block, lambda i: (layer, sub) + zeros)
    return pl.BlockSpec(block, lambda i, j: (layer, sub) + zeros)


def _norm_g_spec(norm_g4, layer, sub):
    return pl.BlockSpec((None, None, 1, norm_g4.shape[-1]), lambda i, j: (layer, sub, 0, 0))


def _ada_kernel(c_ref, w_ref, b_ref, o_ref):
    c = c_ref[...]
    cond = (c * jax.nn.sigmoid(c)).astype(BF16)
    o_ref[...] = _dot(cond, w_ref[...].astype(BF16)) + b_ref[...]


def _ada_table(c_all, ada_w, ada_b):
    depth, d, n = ada_w.shape
    n_mod = n // d
    blocks = 2 * _nbytes((SEQ_PAD, d), F32) + _nbytes((d, d), F32) + _nbytes((SUBLANES, d), F32)
    return pl.pallas_call(
        _ada_kernel,
        grid=(depth, n_mod),
        in_specs=[
            pl.BlockSpec((SEQ_PAD, d), lambda l, j: (0, 0)),
            pl.BlockSpec((None, d, d), lambda l, j: (l, 0, j)),
            pl.BlockSpec((None, None, 1, d), lambda l, j: (l, j, 0, 0)),
        ],
        out_specs=pl.BlockSpec((None, None, SEQ_PAD, d), lambda l, j: (l, j, 0, 0)),
        out_shape=jax.ShapeDtypeStruct((depth, n_mod, SEQ_PAD, d), F32),
        compiler_params=pltpu.CompilerParams(
            dimension_semantics=("parallel", "arbitrary"),
            vmem_limit_bytes=_vmem_limit(blocks, temp_bytes=_nbytes((d, d), BF16))),
        name="ada_table",
    )(c_all, ada_w, ada_b.reshape(depth, n_mod, 1, d))


def _ffn_kernel(x_ref, xa_ref, mod_ref, ng_ref, w1_ref, w3_ref, w2_ref, o_ref, ha_ref, hb_ref, *,
                tm, ahead, lay):
    i = pl.program_id(0)
    j = pl.program_id(1)

    @pl.when(j == 0)
    def _():
        o_ref[...] = x_ref[...]

    @pl.when(jnp.logical_and(i == 0, j == 0))
    def _():
        _norm_mod_rows(ha_ref, x_ref, mod_ref, ng_ref, 0, tm, lay)

    def step(h_ref, h_next):
        h = h_ref[...]
        a = _dot(h, w1_ref[...].astype(BF16))
        b = _dot(h, w3_ref[...].astype(BF16))
        g = (a * jax.nn.sigmoid(a) * b).astype(BF16)
        p = _dot(g, w2_ref[...].astype(BF16))
        for grp in range(tm // CHUNK):
            rows = slice(grp * CHUNK, (grp + 1) * CHUNK)
            gate = MACARON * _gate_row(mod_ref, i * tm + grp * CHUNK, lay)
            o_ref[rows, :] += gate * p[rows, :]
        ahead.run(h_next, xa_ref, mod_ref, ng_ref, i, j, lay)

    _by_parity(i, ha_ref, hb_ref, step)


def _ffn(x, mod, norm_g4, w1, w3, w2, layer, sub, which, lay):
    t, d = x.shape
    f = w1.shape[-1]
    tm = _pick(t, (1024, 768, 512, 384, 256, 128))
    tf = _pick(f, (256, 128))
    nj = f // tf
    ahead = AheadPlan(tm, t // tm, nj)
    blocks = (2 * _nbytes((tm, d), F32) + _nbytes((ahead.rows, d), F32) + _nbytes(mod.shape[2:], F32)
              + 3 * _nbytes((d, tf), w1.dtype))
    temps = 2 * _nbytes((tm, tf), F32) + _nbytes((tm, tf), BF16) + 3 * _nbytes((d, tf), BF16)
    return pl.pallas_call(
        functools.partial(_ffn_kernel, tm=tm, ahead=ahead, lay=lay),
        grid=(t // tm, nj),
        in_specs=[
            pl.BlockSpec((tm, d), lambda i, j: (i, 0)),
            ahead.spec(d),
            _mod_spec(mod, layer, sub, 2),
            _norm_g_spec(norm_g4, layer, sub),
            pl.BlockSpec((None, None, d, tf), lambda i, j: (layer, which, 0, j)),
            pl.BlockSpec((None, None, d, tf), lambda i, j: (layer, which, 0, j)),
            pl.BlockSpec((None, None, tf, d), lambda i, j: (layer, which, j, 0)),
        ],
        out_specs=pl.BlockSpec((tm, d), lambda i, j: (i, 0)),
        out_shape=jax.ShapeDtypeStruct((t, d), F32),
        scratch_shapes=[pltpu.VMEM((tm, d), BF16), pltpu.VMEM((tm, d), BF16)],
        compiler_params=pltpu.CompilerParams(
            dimension_semantics=("arbitrary", "arbitrary"),
            vmem_limit_bytes=_vmem_limit(blocks, _nbytes((2, tm, d), BF16), temps)),
        name="ffn",
    )(x, x, mod, norm_g4, w1, w3, w2)


def _inproj_kernel(x0_ref, xa_ref, mod_ref, ng_ref, *refs, mode, has_bias, tm, ahead, lay):
    i = pl.program_id(0)
    j = pl.program_id(1)
    ha_ref, hb_ref = refs[-2:]

    @pl.when(jnp.logical_and(i == 0, j == 0))
    def _():
        _norm_mod_rows(ha_ref, x0_ref, mod_ref, ng_ref, 0, tm, lay)

    def step(h_ref, h_next):
        h = h_ref[...]
        if mode == "single":
            if has_bias:
                w_ref, b_ref, o_ref = refs[:3]
                o_ref[...] = _dot(h, w_ref[...].astype(BF16)) + b_ref[...]
            else:
                w_ref, o_ref = refs[:2]
                o_ref[...] = _dot(h, w_ref[...].astype(BF16))
        elif mode == "glu":
            wa_ref, wb_ref, ba_ref, bb_ref, o_ref = refs[:5]
            a = _dot(h, wa_ref[...].astype(BF16)) + ba_ref[...]
            b = _dot(h, wb_ref[...].astype(BF16)) + bb_ref[...]
            o_ref[...] = a * jax.nn.sigmoid(b)
        else:
            wa_ref, wb_ref, ba_ref, bb_ref, ou_ref, ov_ref = refs[:6]
            ou_ref[...] = jax.nn.gelu(_dot(h, wa_ref[...].astype(BF16)) + ba_ref[...],
                                      approximate=True).astype(ou_ref.dtype)
            ov_ref[...] = jax.nn.gelu(_dot(h, wb_ref[...].astype(BF16)) + bb_ref[...], approximate=True)
        ahead.run(h_next, xa_ref, mod_ref, ng_ref, i, j, lay)

    _by_parity(i, ha_ref, hb_ref, step)


def _inproj(x, mod, norm_g4, layer, w, b, lay, mode):
    t, d = x.shape
    n = w.shape[1]
    tm = _pick(t, (1024, 768, 512, 384, 256, 128))
    tn = _pick(n if mode == "single" else n // 2, (512, 256, 128))
    nj = (n if mode == "single" else n // 2) // tn
    ahead = AheadPlan(tm, t // tm, nj)
    x_spec = pl.BlockSpec((tm, d), lambda i, j: (0, 0))
    mod_spec = _mod_spec(mod, layer, 1, 2)
    ng_spec = _norm_g_spec(norm_g4, layer, 1)
    lo = pl.BlockSpec((tm, tn), lambda i, j: (i, j))
    if mode == "single":
        ins = [x, x, mod, norm_g4, w]
        specs = [x_spec, ahead.spec(d), mod_spec, ng_spec, pl.BlockSpec((d, tn), lambda i, j: (0, j))]
        if b is not None:
            ins.append(b.reshape(1, n))
            specs.append(pl.BlockSpec((1, tn), lambda i, j: (0, j)))
        out_shape = jax.ShapeDtypeStruct((t, n), F32)
        out_specs = lo
        n_w = 1
    else:
        half = n // 2
        ins = [x, x, mod, norm_g4, w, w, b.reshape(1, n), b.reshape(1, n)]
        specs = [x_spec, ahead.spec(d), mod_spec, ng_spec,
                 pl.BlockSpec((d, tn), lambda i, j: (0, j)),
                 pl.BlockSpec((d, tn), lambda i, j: (0, j + nj)),
                 pl.BlockSpec((1, tn), lambda i, j: (0, j)),
                 pl.BlockSpec((1, tn), lambda i, j: (0, j + nj))]
        n_w = 2
        if mode == "glu":
            out_shape = jax.ShapeDtypeStruct((t, half), F32)
            out_specs = lo
        else:
            out_shape = (jax.ShapeDtypeStruct((t, half), BF16), jax.ShapeDtypeStruct((t, half), F32))
            out_specs = (lo, lo)
    blocks = (_nbytes((tm, d), F32) + _nbytes((ahead.rows, d), F32) + _nbytes(mod.shape[2:], F32)
              + n_w * _nbytes((d, tn), w.dtype) + n_w * _nbytes((tm, tn), F32))
    return pl.pallas_call(
        functools.partial(_inproj_kernel, mode=mode, has_bias=b is not None, tm=tm, ahead=ahead, lay=lay),
        grid=(t // tm, nj),
        in_specs=specs,
        out_specs=out_specs,
        out_shape=out_shape,
        scratch_shapes=[pltpu.VMEM((tm, d), BF16), pltpu.VMEM((tm, d), BF16)],
        compiler_params=pltpu.CompilerParams(
            dimension_semantics=("arbitrary", "arbitrary"),
            vmem_limit_bytes=_vmem_limit(blocks, _nbytes((2, tm, d), BF16),
                                         n_w * (3 * _nbytes((tm, tn), F32) + _nbytes((d, tn), BF16)))),
        name="inproj_" + mode,
    )(*ins)


def _outproj_kernel(x_ref, a_ref, mod_ref, w_ref, *refs, has_bias, tm, nk, lay):
    i = pl.program_id(0)
    k = pl.program_id(1)
    b_ref, o_ref = (refs[0], refs[1]) if has_bias else (None, refs[0])
    groups = tm // CHUNK

    def gates():
        for grp in range(groups):
            yield slice(grp * CHUNK, (grp + 1) * CHUNK), _gate_row(mod_ref, i * tm + grp * CHUNK, lay)

    p = _dot(a_ref[...], w_ref[...])
    if nk == 1:
        for rows, gate in gates():
            acc = p[rows, :] if b_ref is None else p[rows, :] + b_ref[...]
            o_ref[rows, :] = x_ref[rows, :] + gate * acc
    else:
        @pl.when(k == 0)
        def _():
            for rows, gate in gates():
                o_ref[rows, :] = x_ref[rows, :] if b_ref is None else x_ref[rows, :] + gate * b_ref[...]

        for rows, gate in gates():
            o_ref[rows, :] += gate * p[rows, :]


def _outproj(x, a, mod, layer, w, b, lay):
    t, d = x.shape
    kdim = a.shape[1]
    tm = _pick(t, (512, 384, 256, 128))
    tk = _pick(kdim, (2048, 1024, 512))
    nk = kdim // tk
    ins = [x, a, mod, w]
    specs = [pl.BlockSpec((tm, d), lambda i, k: (i, 0)),
             pl.BlockSpec((tm, tk), lambda i, k: (i, k)),
             _mod_spec(mod, layer, 1, 2),
             pl.BlockSpec((tk, d), lambda i, k: (k, 0))]
    if b is not None:
        ins.append(b.reshape(1, d))
        specs.append(pl.BlockSpec((1, d), lambda i, k: (0, 0)))
    blocks = (2 * _nbytes((tm, d), F32) + _nbytes((tm, tk), BF16) + _nbytes(mod.shape[2:], F32)
              + _nbytes((tk, d), BF16))
    return pl.pallas_call(
        functools.partial(_outproj_kernel, has_bias=b is not None, tm=tm, nk=nk, lay=lay),
        grid=(t // tm, nk),
        in_specs=specs,
        out_specs=pl.BlockSpec((tm, d), lambda i, k: (i, 0)),
        out_shape=jax.ShapeDtypeStruct((t, d), F32),
        compiler_params=pltpu.CompilerParams(
            dimension_semantics=("parallel", "arbitrary"),
            vmem_limit_bytes=_vmem_limit(blocks)),
        name="outproj",
    )(*ins)


def _fill_history(ext_ref, prev_ref, cache_ref, n, n_prompt_tiles, tiles_per_seq):
    rows = prev_ref.shape[0]
    is_sample = n >= n_prompt_tiles
    starts_seq = jnp.logical_and(n % tiles_per_seq == 0, jnp.logical_not(is_sample))

    @pl.when(is_sample)
    def _():
        ext_ref[0:rows, :] = cache_ref[...]

    @pl.when(starts_seq)
    def _():
        ext_ref[0:rows, :] = jnp.zeros(prev_ref.shape, ext_ref.dtype)

    @pl.when(jnp.logical_not(jnp.logical_or(is_sample, starts_seq)))
    def _():
        ext_ref[0:rows, :] = prev_ref[...]


def _history_specs(rows, d, n_prompt_tiles, sub=1, s=0):
    per_tile = CHUNK // rows
    return [pl.BlockSpec((None, rows, d), lambda i: (jnp.maximum((i * sub + s) * per_tile - 1, 0), 0, 0)),
            pl.BlockSpec((None, rows, d), lambda i: (jnp.maximum(i * sub + s - n_prompt_tiles, 0), 0, 0))]


def _pad_cache(cache, rows):
    return jnp.pad(cache, ((0, 0), (rows - cache.shape[1], 0), (0, 0)))


def _conv_mid_kernel(g_ref, prev_ref, cache_ref, wdw_ref, bdw_ref, lng_ref, lnb_ref, o_ref, ext_ref, y_ref, *,
                     tt, cb, n_prompt_tiles, tiles_per_seq):
    d = g_ref.shape[-1]
    _fill_history(ext_ref, prev_ref, cache_ref, pl.program_id(0), n_prompt_tiles, tiles_per_seq)
    ext_ref[CONV_HALO:CONV_HALO + tt, :] = g_ref[...]
    ext_ref[CONV_HALO + tt:CONV_HALO + tt + SUBLANES, :] = jnp.zeros((SUBLANES, d), F32)
    first = CONV_HALO - CONV_STATE
    for c0 in range(0, d, cb):
        cols = slice(c0, c0 + cb)
        y = None
        for r in range(SUBLANES):
            u = None
            for k in range(CONV_WIDTH):
                if (first + k) % SUBLANES != r:
                    continue
                a0 = (first + k) - r
                term = ext_ref[a0:a0 + tt + SUBLANES, cols] * wdw_ref[k:k + 1, cols]
                u = term if u is None else u + term
            if u is None:
                continue
            piece = u[r:r + tt, :]
            y = piece if y is None else y + piece
        y_ref[:, cols] = y + bdw_ref[:, cols]
    y = y_ref[...]
    yc = y - jnp.mean(y, axis=-1, keepdims=True)
    var = jnp.mean(yc * yc, axis=-1, keepdims=True)
    z = yc * lax.rsqrt(var + EPS) * lng_ref[...] + lnb_ref[...]
    o_ref[...] = (z * jax.nn.sigmoid(z)).astype(o_ref.dtype)


def _conv_mid(g, cache, w_dw, b_dw, ln_g, ln_b, lay):
    t, d = g.shape
    tt = CHUNK
    n = t // tt
    n_prompt_tiles = lay.tp // tt
    cb = 2 * LANES
    blocks = (_nbytes((tt, d), F32) + 2 * _nbytes((CONV_HALO, d), F32) + _nbytes((32, d), F32)
              + 3 * _nbytes((SUBLANES, d), F32) + _nbytes((tt, d), BF16))
    vec = pl.BlockSpec((1, d), lambda i: (0, 0))
    ext_rows = CONV_HALO + tt + SUBLANES
    return pl.pallas_call(
        functools.partial(_conv_mid_kernel, tt=tt, cb=cb, n_prompt_tiles=n_prompt_tiles,
                          tiles_per_seq=lay.sp // tt),
        grid=(n,),
        in_specs=[pl.BlockSpec((None, tt, d), lambda i: (i, 0, 0))]
        + _history_specs(CONV_HALO, d, n_prompt_tiles) + [
            pl.BlockSpec((CONV_WIDTH, d), lambda i: (0, 0)),
            vec, vec, vec,
        ],
        out_specs=pl.BlockSpec((None, tt, d), lambda i: (i, 0, 0)),
        out_shape=jax.ShapeDtypeStruct((n, tt, d), BF16),
        scratch_shapes=[pltpu.VMEM((ext_rows, d), F32), pltpu.VMEM((tt, d), F32)],
        compiler_params=pltpu.CompilerParams(
            dimension_semantics=("parallel",),
            vmem_limit_bytes=_vmem_limit(blocks, _nbytes((ext_rows + tt, d), F32), 4 * _nbytes((tt, d), F32))),
        name="conv_mid",
    )(g.reshape(n, tt, d), g.reshape(t // CONV_HALO, CONV_HALO, d), _pad_cache(cache, CONV_HALO),
      w_dw, b_dw.reshape(1, d), ln_g.reshape(1, d), ln_b.reshape(1, d))


def _pool_mid_kernel(p_ref, *refs, sub, n_prompt_tiles, tiles_per_seq):
    hist_refs = refs[:2 * sub]
    wg_ref, sc_ref, o_ref, ext_ref, pooled_ref = refs[2 * sub:]
    n = pl.program_id(0)
    d = p_ref.shape[-1]
    gw = d // len(POOL_WINDOWS)
    for s in range(sub):
        tile = n * sub + s
        rows = slice(s * CHUNK, (s + 1) * CHUNK)
        _fill_history(ext_ref, hist_refs[2 * s], hist_refs[2 * s + 1], tile, n_prompt_tiles, tiles_per_seq)
        ext_ref[POOL_HALO:POOL_HALO + CHUNK, :] = p_ref[rows, :]
        pos0 = jnp.where(tile < n_prompt_tiles, (tile % tiles_per_seq) * CHUNK, PAST_LEN)
        pos = pos0 + lax.broadcasted_iota(jnp.int32, (CHUNK, 1), 0)
        for gi, w in enumerate(POOL_WINDOWS):
            cols = slice(gi * gw, (gi + 1) * gw)
            acc = ext_ref[POOL_HALO:POOL_HALO + CHUNK, cols]
            for back in range(1, w):
                acc = acc + ext_ref[POOL_HALO - back:POOL_HALO - back + CHUNK, cols]
            inv_cnt = 1.0 / jnp.minimum(pos + 1, w).astype(F32)
            pooled_ref[rows, cols] = (acc * inv_cnt - p_ref[rows, cols]).astype(pooled_ref.dtype)
    for gi in range(len(POOL_WINDOWS)):
        cols = slice(gi * gw, (gi + 1) * gw)
        z = _dot(pooled_ref[:, cols], wg_ref[gi]) * sc_ref[:, cols]
        o_ref[:, cols] = z.astype(o_ref.dtype)


def _pool_mid(p, cache, w_grp, scale, lay):
    t, d = p.shape
    sub = 4
    tt = sub * CHUNK
    assert lay.tp % tt == 0 and lay.tsamp % tt == 0
    n_prompt_tiles = lay.tp // CHUNK
    hist = []
    hist_in = []
    for s in range(sub):
        hist += _history_specs(POOL_HALO, d, n_prompt_tiles, sub, s)
        hist_in += [p.reshape(t // POOL_HALO, POOL_HALO, d), _pad_cache(cache, POOL_HALO)]
    blocks = (_nbytes((tt, d), F32) + 2 * sub * _nbytes((POOL_HALO, d), F32) + _nbytes(w_grp.shape, BF16)
              + _nbytes((SUBLANES, d), F32) + _nbytes((tt, d), BF16))
    return pl.pallas_call(
        functools.partial(_pool_mid_kernel, sub=sub, n_prompt_tiles=n_prompt_tiles,
                          tiles_per_seq=lay.sp // CHUNK),
        grid=(t // tt,),
        in_specs=[pl.BlockSpec((tt, d), lambda i: (i, 0))] + hist + [
            pl.BlockSpec(w_grp.shape, lambda i: (0, 0, 0)),
            pl.BlockSpec((1, d), lambda i: (0, 0)),
        ],
        out_specs=pl.BlockSpec((tt, d), lambda i: (i, 0)),
        out_shape=jax.ShapeDtypeStruct((t, d), BF16),
        scratch_shapes=[pltpu.VMEM((POOL_HALO + CHUNK, d), F32), pltpu.VMEM((tt, d), BF16)],
        compiler_params=pltpu.CompilerParams(
            dimension_semantics=("parallel",),
            vmem_limit_bytes=_vmem_limit(blocks, _nbytes((POOL_HALO + CHUNK, d), F32) + _nbytes((tt, d), BF16),
                                         4 * _nbytes((tt, d), F32))),
        name="pool_mid",
    )(p, *hist_in, w_grp, scale.reshape(1, d))


def _attn_kernel(q_ref, k0_ref, k1_ref, k2_ref, v0_ref, v1_ref, v2_ref, bias_ref, sink_ref, o_ref):
    keys = SWA_BAND + CHUNK
    pair_heads = Q_PER_KV // 2
    low_half = lax.broadcasted_iota(jnp.int32, (keys, LANES), 1) < HEAD_DIM
    tn_dims = (((1,), (1,)), ((), ()))
    nt_dims = (((0,), (0,)), ((), ()))

    units = []
    for m in range(N_KV_HEADS // 2):
        cols = slice(m * LANES, (m + 1) * LANES)
        kb = jnp.concatenate([k0_ref[:, cols], k1_ref[:, cols], k2_ref[:, cols]], axis=0) * ATTN_SCALE
        vb = jnp.concatenate([v0_ref[:, cols], v1_ref[:, cols], v2_ref[:, cols]], axis=0)
        kswap = pltpu.roll(kb, HEAD_DIM, axis=1)
        vswap = pltpu.roll(vb, HEAD_DIM, axis=1)
        for sub in range(2):
            kh = 2 * m + sub
            k_low, k_high = (kb, kswap) if sub == 0 else (kswap, kb)
            v_low, v_high = (vb, vswap) if sub == 0 else (vswap, vb)
            units.append((kh, 0, jnp.where(low_half, k_low, 0.0).astype(BF16),
                          jnp.where(low_half, v_low, 0.0).astype(BF16)))
            units.append((kh, 1, jnp.where(low_half, 0.0, k_high).astype(BF16),
                          jnp.where(low_half, 0.0, v_high).astype(BF16)))

    qstacks = []
    for kh in range(N_KV_HEADS):
        qstacks.append(jnp.concatenate(
            [q_ref[:, (kh * pair_heads + j) * LANES:(kh * pair_heads + j + 1) * LANES] for j in range(pair_heads)],
            axis=0).astype(BF16))

    scores = []
    for kh, par, kpad, _ in units:
        s = lax.dot_general(kpad, qstacks[kh], tn_dims, preferred_element_type=F32)
        scores.append(s + bias_ref[2 * kh + par])

    probs = []
    for (kh, par, _, _), s in zip(units, scores):
        sk = sink_ref[2 * kh + par]
        mx = jnp.maximum(jnp.max(s, axis=0, keepdims=True), sk)
        e = jnp.exp(s - mx)
        denom = jnp.sum(e, axis=0, keepdims=True) + jnp.exp(sk - mx)
        probs.append((e * (1.0 / denom)).astype(BF16))

    for kh in range(N_KV_HEADS):
        acc = None
        for (ukh, par, _, vpad), p in zip(units, probs):
            if ukh != kh:
                continue
            o = lax.dot_general(p, vpad, nt_dims, preferred_element_type=F32)
            acc = o if acc is None else acc + o
        for j in range(pair_heads):
            c0 = (kh * pair_heads + j) * LANES
            o_ref[:, c0:c0 + LANES] = acc[j * CHUNK:(j + 1) * CHUNK, :].astype(o_ref.dtype)


def _attention(qkv, kpad, vpad, bias, sinks, lay):
    t = qkv.shape[0]
    hw = N_HEADS * HEAD_DIM
    kvw = N_KV_HEADS * HEAD_DIM
    n_hist = SWA_BAND // CHUNK
    cps = lay.sp // CHUNK
    n_prompt_chunks = lay.bp * cps

    def kv_map(off):
        def index(c):
            base = jnp.where(c < n_prompt_chunks,
                             (c // cps) * (cps + n_hist) + c % cps,
                             lay.bp * (cps + n_hist) + (c - n_prompt_chunks) * (n_hist + 1))
            return (base + off, 0)
        return index

    def bias_map(c):
        return (jnp.where(c < n_prompt_chunks, jnp.minimum(c % cps, n_hist), n_hist), 0, 0, 0)

    kv_specs = [pl.BlockSpec((CHUNK, kvw), kv_map(o)) for o in range(n_hist + 1)]
    blocks = (_nbytes((CHUNK, hw), F32) + 6 * _nbytes((CHUNK, kvw), F32) + _nbytes(bias.shape[1:], F32)
              + _nbytes((2 * N_KV_HEADS, SUBLANES, bias.shape[-1]), F32) + _nbytes((CHUNK, hw), BF16))
    return pl.pallas_call(
        _attn_kernel,
        grid=(t // CHUNK,),
        in_specs=[pl.BlockSpec((CHUNK, hw), lambda c: (c, 0))] + kv_specs + kv_specs + [
            pl.BlockSpec((None,) + bias.shape[1:], bias_map),
            pl.BlockSpec(sinks.shape, lambda c: (0, 0, 0)),
        ],
        out_specs=pl.BlockSpec((CHUNK, hw), lambda c: (c, 0)),
        out_shape=jax.ShapeDtypeStruct((t, hw), BF16),
        compiler_params=pltpu.CompilerParams(
            dimension_semantics=("parallel",),
            vmem_limit_bytes=_vmem_limit(blocks, 0, 6 * _nbytes(bias.shape[1:], F32))),
        name="swa_attention",
    )(qkv, kpad, kpad, kpad, vpad, vpad, vpad, bias, sinks)


def _alibi_tables(swa_sinks):
    n_hist = SWA_BAND // CHUNK
    keys = SWA_BAND + CHUNK
    pair_heads = Q_PER_KV // 2

    def by_unit(per_head):
        return per_head.reshape(N_KV_HEADS, pair_heads, 2).transpose(0, 2, 1).reshape(2 * N_KV_HEADS, pair_heads)

    slopes = by_unit(jnp.exp2(-8.0 * jnp.arange(1, N_HEADS + 1, dtype=F32) / N_HEADS))
    dist = jnp.abs(jnp.arange(CHUNK)[None, :] + SWA_BAND - jnp.arange(keys)[:, None]).astype(F32)
    bias = (-slopes[:, None, :, None] * dist[None, :, None, :]).reshape(2 * N_KV_HEADS, keys, pair_heads * CHUNK)
    key_idx = jnp.arange(keys)[None, :, None]
    bias = jnp.stack([jnp.where(key_idx >= (n_hist - v) * CHUNK, bias, -jnp.inf) for v in range(n_hist + 1)])
    sinks = jnp.broadcast_to(by_unit(swa_sinks.astype(F32))[:, None, :, None],
                             (2 * N_KV_HEADS, 1, pair_heads, CHUNK)).reshape(2 * N_KV_HEADS, 1, pair_heads * CHUNK)
    return bias, sinks


def _gmlp_mid_kernel(u_ref, v_ref, lng_ref, lnb_ref, ws_ref, bs_ref, o_ref, vn_ref, *, n_prompt_blocks):
    n = pl.program_id(0)
    is_sample = n >= n_prompt_blocks
    v = v_ref[...]
    vc = v - jnp.mean(v, axis=-1, keepdims=True)
    var = jnp.mean(vc * vc, axis=-1, keepdims=True)
    vn = vc * lax.rsqrt(var + EPS) * lng_ref[...] + lnb_ref[...]
    vn_ref[...] = vn
    vnb = vn.astype(BF16)
    ll = GMLP_CHUNK
    rb = lax.broadcasted_iota(jnp.int32, (ll, ll), 0) // CHUNK
    cb = lax.broadcasted_iota(jnp.int32, (ll, ll), 1) // CHUNK
    back = rb - cb
    mask = (back >= 0) & (back <= jnp.where(is_sample, 0, 1))
    gw = u_ref.shape[-1] // GMLP_GROUPS
    for g in range(GMLP_GROUPS):
        w = jnp.where(mask, ws_ref[g], 0.0).astype(BF16)
        mixed = _dot(w, vnb[:, g * gw:(g + 1) * gw]) + bs_ref[g]
        o_ref[:, g * gw:(g + 1) * gw] = (u_ref[:, g * gw:(g + 1) * gw].astype(F32) * mixed).astype(o_ref.dtype)


def _gmlp_mid(u, v, ln_g, ln_b, ws2, bs2, n_prompt_blocks):
    t, gwid = u.shape
    ll = GMLP_CHUNK
    sel = lambda n: jnp.where(n >= n_prompt_blocks, 1, 0)
    blocks = (_nbytes((ll, gwid), BF16) + 2 * _nbytes((ll, gwid), F32) + 2 * _nbytes((SUBLANES, gwid), F32)
              + _nbytes((GMLP_GROUPS, ll, ll), F32) + _nbytes((GMLP_GROUPS, ll, LANES), F32)
              + _nbytes((ll, gwid), BF16))
    return pl.pallas_call(
        functools.partial(_gmlp_mid_kernel, n_prompt_blocks=n_prompt_blocks),
        grid=(t // ll,),
        in_specs=[
            pl.BlockSpec((ll, gwid), lambda n: (n, 0)),
            pl.BlockSpec((ll, gwid), lambda n: (n, 0)),
            pl.BlockSpec((1, gwid), lambda n: (0, 0)),
            pl.BlockSpec((1, gwid), lambda n: (0, 0)),
            pl.BlockSpec((None, GMLP_GROUPS, ll, ll), lambda n: (sel(n), 0, 0, 0)),
            pl.BlockSpec((None, GMLP_GROUPS, ll, 1), lambda n: (sel(n), 0, 0, 0)),
        ],
        out_specs=(pl.BlockSpec((ll, gwid), lambda n: (n, 0)),
                   pl.BlockSpec((ll, gwid), lambda n: (jnp.maximum(n - n_prompt_blocks, 0), 0))),
        out_shape=(jax.ShapeDtypeStruct((t, gwid), BF16),
                   jax.ShapeDtypeStruct((t - n_prompt_blocks * ll, gwid), F32)),
        compiler_params=pltpu.CompilerParams(
            dimension_semantics=("arbitrary",),
            vmem_limit_bytes=_vmem_limit(blocks, 0, 6 * _nbytes((ll, gwid), F32))),
        name="gmlp_mid",
    )(u, v, ln_g.reshape(1, gwid), ln_b.reshape(1, gwid), ws2, bs2)


def _final_norm_kernel(x_ref, g_ref, op_ref, os_ref, *, n_prompt_tiles):
    x = x_ref[...]
    ms = jnp.mean(x * x, axis=-1, keepdims=True)
    y = x * lax.rsqrt(ms + EPS) * g_ref[...]
    os_ref[...] = y

    @pl.when(pl.program_id(0) < n_prompt_tiles)
    def _():
        op_ref[...] = y


def _final_norm(x, g, lay):
    t, d = x.shape
    tm = _pick(lay.tp, (256, 128))
    assert lay.tsamp % tm == 0
    npt = lay.tp // tm
    return pl.pallas_call(
        functools.partial(_final_norm_kernel, n_prompt_tiles=npt),
        grid=(t // tm,),
        in_specs=[pl.BlockSpec((tm, d), lambda i: (i, 0)), pl.BlockSpec((1, d), lambda i: (0, 0))],
        out_specs=(pl.BlockSpec((tm, d), lambda i: (jnp.minimum(i, npt - 1), 0)),
                   pl.BlockSpec((tm, d), lambda i: (jnp.maximum(i - npt, 0), 0))),
        out_shape=(jax.ShapeDtypeStruct((lay.tp, d), F32), jax.ShapeDtypeStruct((lay.tsamp, d), F32)),
        compiler_params=pltpu.CompilerParams(
            dimension_semantics=("arbitrary",),
            vmem_limit_bytes=_vmem_limit(3 * _nbytes((tm, d), F32), 0, 2 * _nbytes((tm, d), F32))),
        name="final_norm",
    )(x, g.reshape(1, d))


def _state_tail(y, cache, keep, lay):
    d = y.shape[-1]
    tails = jnp.stack([y[(b + 1) * lay.sp - keep:(b + 1) * lay.sp] for b in range(lay.bp)])
    ys = y[lay.tp:].reshape(lay.bs, lay.ts, d)
    return tails, jnp.concatenate([cache, ys], axis=1)[:, -keep:]


def kernel(x_prompt, x_sample, c_prompt, c_sample, cache_conv, cache_pool, cache_swa_k, cache_swa_v, ada_w, ada_b, norm_g, ffn_w1, ffn_w3, ffn_w2, conv_w_pw1, conv_b_pw1, conv_w_dw, conv_b_dw, conv_ln_g, conv_ln_b, conv_w_pw2, conv_b_pw2, pool_w_in, pool_w_grp, pool_scale, pool_w_out, swa_wq, swa_wk, swa_wv, swa_wo, swa_sinks, gmlp_w_in, gmlp_b_in, gmlp_ln_g, gmlp_ln_b, gmlp_w_s, gmlp_b_s, gmlp_w_out, gmlp_b_out, final_g):
    bp, sp, d = x_prompt.shape
    bs, ts, _ = x_sample.shape
    lay = Layout(bp, sp, bs, ts)
    depth = ada_w.shape[0]
    kvw = N_KV_HEADS * HEAD_DIM
    hw = N_HEADS * HEAD_DIM

    x = jnp.concatenate([x_prompt.reshape(lay.tp, d), x_sample.reshape(lay.tsamp, d)], axis=0)
    c_all = jnp.pad(jnp.concatenate([c_prompt, c_sample], axis=0), ((0, SEQ_PAD - bp - bs), (0, 0)))
    mod = _ada_table(c_all, ada_w, ada_b).reshape(depth, 3, 3, SEQ_PAD, d)
    norm_g4 = norm_g.reshape(depth, 3, 1, d)

    w1b, w3b, w2b = ffn_w1, ffn_w3, ffn_w2

    outs = {}
    for i in range(depth):
        x = _ffn(x, mod, norm_g4, w1b, w3b, w2b, i, 0, 0, lay)
        kind = i % 4
        if kind == 0:
            g = _inproj(x, mod, norm_g4, i, conv_w_pw1, conv_b_pw1, lay, "glu")
            a = _conv_mid(g, cache_conv, conv_w_dw, conv_b_dw, conv_ln_g, conv_ln_b, lay)
            x = _outproj(x, a.reshape(lay.t, d), mod, i, conv_w_pw2.astype(BF16), conv_b_pw2, lay)
            outs["conv"] = _state_tail(g, cache_conv, CONV_STATE, lay)
        elif kind == 1:
            p = _inproj(x, mod, norm_g4, i, pool_w_in, None, lay, "single")
            a = _pool_mid(p, cache_pool, pool_w_grp.astype(BF16), pool_scale, lay)
            x = _outproj(x, a.reshape(lay.t, d), mod, i, pool_w_out.astype(BF16), None, lay)
            outs["pool"] = _state_tail(p, cache_pool, POOL_STATE, lay)
        elif kind == 2:
            w_qkv = jnp.concatenate([swa_wq, swa_wk, swa_wv], axis=1)
            qkv = _inproj(x, mod, norm_g4, i, w_qkv, None, lay, "single")
            r = cache_swa_k.shape[1]
            assert r == SWA_BAND

            def padded(col0, cache):
                new = qkv[:, col0:col0 + kvw]
                newp = jnp.pad(new[:lay.tp].reshape(bp, sp, kvw), ((0, 0), (SWA_BAND, 0), (0, 0)))
                news = jnp.concatenate([cache.reshape(bs, r, kvw), new[lay.tp:].reshape(bs, ts, kvw)], axis=1)
                tail_p = new[:lay.tp].reshape(bp, sp, kvw)[:, sp - SWA_BAND:]
                tail_s = news[:, ts:]
                both = jnp.concatenate([newp.reshape(-1, kvw), news.reshape(-1, kvw)], axis=0)
                return both, tail_p.reshape(bp, SWA_BAND, N_KV_HEADS, HEAD_DIM), tail_s.reshape(bs, r, N_KV_HEADS, HEAD_DIM)

            kpad, k_p, k_s = padded(hw, cache_swa_k)
            vpad, v_p, v_s = padded(hw + kvw, cache_swa_v)
            bias, sinks = _alibi_tables(swa_sinks)
            a = _attention(qkv, kpad, vpad, bias, sinks, lay)
            x = _outproj(x, a, mod, i, swa_wo.astype(BF16), None, lay)
            outs["kv"] = (k_p, v_p, k_s, v_s)
        else:
            u, v = _inproj(x, mod, norm_g4, i, gmlp_w_in, gmlp_b_in, lay, "gelu2")
            ll = GMLP_CHUNK
            ws2 = jnp.stack([gmlp_w_s[:, :ll, :ll], jnp.tile(gmlp_w_s[:, :ts, :ts], (1, ll // ts, ll // ts))])
            bs2 = jnp.stack([gmlp_b_s[:, :ll], jnp.tile(gmlp_b_s[:, :ts], (1, ll // ts))])[..., None]
            a, vn = _gmlp_mid(u, v, gmlp_ln_g, gmlp_ln_b, ws2, bs2, lay.tp // ll)
            x = _outproj(x, a, mod, i, gmlp_w_out.astype(BF16), gmlp_b_out, lay)
            outs["gmlp_v"] = vn.reshape(bs, ts, -1)
        x = _ffn(x, mod, norm_g4, w1b, w3b, w2b, i, 2, 1, lay)

    y_p, y_s = _final_norm(x, final_g, lay)
    y_prompt = y_p.reshape(bp, sp, d)
    y_sample = y_s.reshape(bs, ts, d)
    conv_p, conv_s = outs["conv"]
    pool_p, pool_s = outs["pool"]
    k_p, v_p, k_s, v_s = outs["kv"]
    return (y_prompt, y_sample, conv_p, conv_s, pool_p, pool_s, k_p, v_p, k_s, v_s, outs["gmlp_v"])
```

```python
import functools

import jax
import jax.numpy as jnp
from jax import lax
from jax.experimental import pallas as pl
from jax.experimental.pallas import tpu as pltpu

F32 = jnp.float32
BF16 = jnp.bfloat16

EPS = 1e-6
MACARON = 0.5
N_MOD = 9
CONV_WIDTH = 31
CONV_STATE = CONV_WIDTH - 1
POOL_WINDOWS = (2, 4, 8, 16)
POOL_STATE = max(POOL_WINDOWS) - 1
HEAD_DIM = 64
N_KV_HEADS = 4
Q_PER_KV = 8
N_HEADS = N_KV_HEADS * Q_PER_KV
CHUNK = 64
SWA_BAND = 128
ATTN_SCALE = HEAD_DIM ** -0.5
GMLP_CHUNK = 128
GMLP_GROUPS = 8
PAST_LEN = 1024

LANES = 128
SUBLANES = 8
ROW_CHUNK = 16
CONV_HALO = 32
POOL_HALO = 16
SEQ_PAD = 32

V7X_VMEM_LIMIT_CAP = 60 * 1024 * 1024


class Layout:
    def __init__(self, bp, sp, bs, ts):
        self.bp, self.sp, self.bs, self.ts = bp, sp, bs, ts
        self.tp = bp * sp
        self.tsamp = bs * ts
        self.t = self.tp + self.tsamp
        assert sp % GMLP_CHUNK == 0 and ts == CHUNK and self.tsamp % GMLP_CHUNK == 0
        assert bp + bs <= SEQ_PAD

    def seq_of(self, row0):
        return jnp.where(row0 < self.tp, row0 // self.sp, self.bp + (row0 - self.tp) // self.ts)


def _pick(total, candidates):
    for c in candidates:
        if total % c == 0:
            return c
    raise ValueError(f"no tile for {total} in {candidates}")


def _vmem_limit(pipelined_bytes, scratch_bytes=0, temp_bytes=0):
    est = 2 * pipelined_bytes + scratch_bytes + temp_bytes + (4 << 20)
    return int(min(max(est, 16 << 20), V7X_VMEM_LIMIT_CAP))


def _nbytes(shape, dtype):
    n = 1
    for s in shape:
        n *= s
    return n * jnp.dtype(dtype).itemsize


def _dot(a, b):
    return jnp.dot(a, b, preferred_element_type=F32)


def _norm_mod_group(h_ref, h_row, x_ref, x_row, mod_ref, ng, global_row, lay):
    seq = lay.seq_of(global_row)
    shift = mod_ref[0, pl.ds(seq, 1), :]
    gain = ng * (1.0 + mod_ref[1, pl.ds(seq, 1), :])
    for s in range(CHUNK // ROW_CHUNK):
        xg = x_ref[pl.ds(x_row + s * ROW_CHUNK, ROW_CHUNK), :]
        ms = jnp.mean(xg * xg, axis=-1, keepdims=True)
        h_ref[pl.ds(h_row + s * ROW_CHUNK, ROW_CHUNK), :] = (
            xg * lax.rsqrt(ms + EPS) * gain + shift).astype(h_ref.dtype)


def _norm_mod_rows(h_ref, x_ref, mod_ref, ng_ref, row_base, tm, lay):
    ng = ng_ref[...]

    def body(c, carry):
        r = pl.multiple_of(c * CHUNK, CHUNK)
        _norm_mod_group(h_ref, r, x_ref, r, mod_ref, ng, row_base + r, lay)
        return carry

    lax.fori_loop(0, tm // CHUNK, body, 0)


class AheadPlan:
    def __init__(self, tm, n_tiles, n_steps):
        self.tm, self.n_tiles = tm, n_tiles
        groups = tm // CHUNK
        need = -(-groups // n_steps)
        self.per_step = next(g for g in range(need, groups + 1) if groups % g == 0)
        self.blocks = groups // self.per_step
        self.rows = self.per_step * CHUNK

    def spec(self, d):
        def index(i, j):
            nxt = jnp.minimum(i + 1, self.n_tiles - 1)
            return (nxt * self.blocks + jnp.minimum(j, self.blocks - 1), 0)
        return pl.BlockSpec((self.rows, d), index)

    def current_spec(self, d):
        return pl.BlockSpec((self.rows, d), lambda i, j: (i * self.blocks + jnp.minimum(j, self.blocks - 1), 0))

    def run(self, h_next, xa_ref, mod_ref, ng_ref, i, j, lay):
        ng = ng_ref[...]
        nxt = jnp.minimum(i + 1, self.n_tiles - 1)
        row0 = pl.multiple_of(jnp.minimum(j, self.blocks - 1) * self.rows, CHUNK)
        for g in range(self.per_step):
            _norm_mod_group(h_next, row0 + g * CHUNK, xa_ref, g * CHUNK, mod_ref, ng,
                            nxt * self.tm + row0 + g * CHUNK, lay)


def _by_parity(i, ha_ref, hb_ref, body):
    @pl.when(i % 2 == 0)
    def _():
        body(ha_ref, hb_ref)

    @pl.when(i % 2 == 1)
    def _():
        body(hb_ref, ha_ref)


def _gate_row(mod_ref, global_row, lay):
    return mod_ref[2, pl.ds(lay.seq_of(global_row), 1), :]


def _first_tile_kernel(x_ref, mod_ref, ng_ref, o_ref, *, rows, lay):
    _norm_mod_rows(o_ref, x_ref, mod_ref, ng_ref, pl.program_id(0) * rows, rows, lay)


def _first_tile_h(x, mod, norm_g4, layer, sub, tm, lay):
    d = x.shape[1]
    rows = _pick(tm, (256, 128))
    return pl.pallas_call(
        functools.partial(_first_tile_kernel, rows=rows, lay=lay),
        grid=(tm // rows,),
        in_specs=[pl.BlockSpec((rows, d), lambda i: (i, 0)),
                  _mod_spec(mod, layer, sub, 1),
                  pl.BlockSpec((None, None, 1, d), lambda i: (layer, sub, 0, 0))],
        out_specs=pl.BlockSpec((rows, d), lambda i: (i, 0)),
        out_shape=jax.ShapeDtypeStruct((tm, d), BF16),
        compiler_params=pltpu.CompilerParams(
            dimension_semantics=("parallel",),
            vmem_limit_bytes=_vmem_limit(2 * _nbytes((rows, d), F32) + _nbytes(mod.shape[2:], F32))),
        name="first_tile_norm",
    )(x, mod, norm_g4)


def _mod_spec(mod, layer, sub, grid_rank):
    zeros = (0,) * 3
    block = (None, None) + mod.shape[2:]
    if grid_rank == 1:
        return pl.BlockSpec(block, lambda i: (layer, sub) + zeros)
    return pl.BlockSpec(block, lambda i, j: (layer, sub) + zeros)


def _norm_g_spec(norm_g4, layer, sub):
    return pl.BlockSpec((None, None, 1, norm_g4.shape[-1]), lambda i, j: (layer, sub, 0, 0))


def _ada_kernel(c_ref, w_ref, b_ref, o_ref):
    c = c_ref[...]
    cond = (c * jax.nn.sigmoid(c)).astype(BF16)
    o_ref[...] = _dot(cond, w_ref[...].astype(BF16)) + b_ref[...]


def _ada_table(c_all, ada_w, ada_b):
    depth, d, n = ada_w.shape
    n_mod = n // d
    blocks = 2 * _nbytes((SEQ_PAD, d), F32) + _nbytes((d, d), F32) + _nbytes((SUBLANES, d), F32)
    return pl.pallas_call(
        _ada_kernel,
        grid=(depth, n_mod),
        in_specs=[
            pl.BlockSpec((SEQ_PAD, d), lambda l, j: (0, 0)),
            pl.BlockSpec((None, d, d), lambda l, j: (l, 0, j)),
            pl.BlockSpec((None, None, 1, d), lambda l, j: (l, j, 0, 0)),
        ],
        out_specs=pl.BlockSpec((None, None, SEQ_PAD, d), lambda l, j: (l, j, 0, 0)),
        out_shape=jax.ShapeDtypeStruct((depth, n_mod, SEQ_PAD, d), F32),
        compiler_params=pltpu.CompilerParams(
            dimension_semantics=("parallel", "arbitrary"),
            vmem_limit_bytes=_vmem_limit(blocks, temp_bytes=_nbytes((d, d), BF16))),
        name="ada_table",
    )(c_all, ada_w, ada_b.reshape(depth, n_mod, 1, d))


def _ffn_kernel(h0_ref, xc_ref, xa_ref, mod_ref, ng_ref, w1_ref, w3_ref, w2_ref, o_ref, ha_ref, hb_ref, *,
                tm, sub_cols, ahead, lay):
    i = pl.program_id(0)
    j = pl.program_id(1)

    @pl.when(jnp.logical_and(i == 0, j == 0))
    def _():
        pltpu.sync_copy(h0_ref, ha_ref)

    @pl.when(j == 0)
    def _():
        o_ref[...] = jnp.zeros_like(o_ref)

    def step(h_ref, h_next):
        row0 = pl.multiple_of(jnp.minimum(j, ahead.blocks - 1) * ahead.rows, CHUNK)
        o_ref[pl.ds(row0, ahead.rows), :] += jnp.where(j < ahead.blocks, xc_ref[...], 0.0)
        h = h_ref[...]
        subs = [slice(c0, c0 + sub_cols) for c0 in range(0, w1_ref.shape[-1], sub_cols)]

        def up_dots(cols):
            return _dot(h, w1_ref[:, cols].astype(BF16)), _dot(h, w3_ref[:, cols].astype(BF16))

        ahead_ab = up_dots(subs[0])
        for s, cols in enumerate(subs):
            a, b = ahead_ab
            if s + 1 < len(subs):
                ahead_ab = up_dots(subs[s + 1])
            g = (a * jax.nn.sigmoid(a) * b).astype(BF16)
            p = _dot(g, w2_ref[cols, :].astype(BF16))
            for grp in range(tm // CHUNK):
                rows = slice(grp * CHUNK, (grp + 1) * CHUNK)
                gate = MACARON * _gate_row(mod_ref, i * tm + grp * CHUNK, lay)
                o_ref[rows, :] += gate * p[rows, :]
        ahead.run(h_next, xa_ref, mod_ref, ng_ref, i, j, lay)

    _by_parity(i, ha_ref, hb_ref, step)


def _ffn(x, mod, norm_g4, w1, w3, w2, layer, sub, which, lay):
    t, d = x.shape
    f = w1.shape[-1]
    tm = _pick(t, (1024, 768, 512, 384, 256, 128))
    tf = _pick(f, (512, 256, 128))
    nj = f // tf
    ahead = AheadPlan(tm, t // tm, nj)
    h0 = _first_tile_h(x, mod, norm_g4, layer, sub, tm, lay)
    blocks = (_nbytes((tm, d), F32) + 2 * _nbytes((ahead.rows, d), F32) + _nbytes(mod.shape[2:], F32)
              + 3 * _nbytes((d, tf), w1.dtype))
    temps = 2 * _nbytes((tm, tf), F32) + _nbytes((tm, tf), BF16) + 3 * _nbytes((d, tf), BF16)
    return pl.pallas_call(
        functools.partial(_ffn_kernel, tm=tm, sub_cols=_pick(tf, (256, 128)), ahead=ahead, lay=lay),
        grid=(t // tm, nj),
        in_specs=[
            pl.BlockSpec(memory_space=pl.ANY),
            ahead.current_spec(d),
            ahead.spec(d),
            _mod_spec(mod, layer, sub, 2),
            _norm_g_spec(norm_g4, layer, sub),
            pl.BlockSpec((None, None, d, tf), lambda i, j: (layer, which, 0, j)),
            pl.BlockSpec((None, None, d, tf), lambda i, j: (layer, which, 0, j)),
            pl.BlockSpec((None, None, tf, d), lambda i, j: (layer, which, j, 0)),
        ],
        out_specs=pl.BlockSpec((tm, d), lambda i, j: (i, 0)),
        out_shape=jax.ShapeDtypeStruct((t, d), F32),
        scratch_shapes=[pltpu.VMEM((tm, d), BF16), pltpu.VMEM((tm, d), BF16)],
        compiler_params=pltpu.CompilerParams(
            dimension_semantics=("arbitrary", "arbitrary"),
            vmem_limit_bytes=_vmem_limit(blocks, _nbytes((2, tm, d), BF16), temps)),
        name="ffn",
    )(h0, x, x, mod, norm_g4, w1, w3, w2)


def _inproj_kernel(h0_ref, xa_ref, mod_ref, ng_ref, *refs, mode, has_bias, tm, ahead, lay):
    i = pl.program_id(0)
    j = pl.program_id(1)
    ha_ref, hb_ref = refs[-2:]

    @pl.when(jnp.logical_and(i == 0, j == 0))
    def _():
        pltpu.sync_copy(h0_ref, ha_ref)

    def step(h_ref, h_next):
        h = h_ref[...]
        if mode == "single":
            if has_bias:
                w_ref, b_ref, o_ref = refs[:3]
                o_ref[...] = _dot(h, w_ref[...].astype(BF16)) + b_ref[...]
            else:
                w_ref, o_ref = refs[:2]
                o_ref[...] = _dot(h, w_ref[...].astype(BF16))
        elif mode == "glu":
            wa_ref, wb_ref, ba_ref, bb_ref, o_ref = refs[:5]
            a = _dot(h, wa_ref[...].astype(BF16)) + ba_ref[...]
            b = _dot(h, wb_ref[...].astype(BF16)) + bb_ref[...]
            o_ref[...] = a * jax.nn.sigmoid(b)
        else:
            wa_ref, wb_ref, ba_ref, bb_ref, ou_ref, ov_ref = refs[:6]
            ou_ref[...] = jax.nn.gelu(_dot(h, wa_ref[...].astype(BF16)) + ba_ref[...],
                                      approximate=True).astype(ou_ref.dtype)
            ov_ref[...] = jax.nn.gelu(_dot(h, wb_ref[...].astype(BF16)) + bb_ref[...], approximate=True)
        ahead.run(h_next, xa_ref, mod_ref, ng_ref, i, j, lay)

    _by_parity(i, ha_ref, hb_ref, step)


def _inproj(x, mod, norm_g4, layer, w, b, lay, mode):
    t, d = x.shape
    n = w.shape[1]
    tm = _pick(t, (1536, 1024, 768, 512, 384, 256, 128))
    tn = _pick(n if mode == "single" else n // 2, (512, 256, 128))
    nj = (n if mode == "single" else n // 2) // tn
    ahead = AheadPlan(tm, t // tm, nj)
    h0 = _first_tile_h(x, mod, norm_g4, layer, 1, tm, lay)
    x_spec = pl.BlockSpec(memory_space=pl.ANY)
    mod_spec = _mod_spec(mod, layer, 1, 2)
    ng_spec = _norm_g_spec(norm_g4, layer, 1)
    lo = pl.BlockSpec((tm, tn), lambda i, j: (i, j))
    if mode == "single":
        ins = [h0, x, mod, norm_g4, w]
        specs = [x_spec, ahead.spec(d), mod_spec, ng_spec, pl.BlockSpec((d, tn), lambda i, j: (0, j))]
        if b is not None:
            ins.append(b.reshape(1, n))
            specs.append(pl.BlockSpec((1, tn), lambda i, j: (0, j)))
        out_shape = jax.ShapeDtypeStruct((t, n), F32)
        out_specs = lo
        n_w = 1
    else:
        half = n // 2
        ins = [h0, x, mod, norm_g4, w, w, b.reshape(1, n), b.reshape(1, n)]
        specs = [x_spec, ahead.spec(d), mod_spec, ng_spec,
                 pl.BlockSpec((d, tn), lambda i, j: (0, j)),
                 pl.BlockSpec((d, tn), lambda i, j: (0, j + nj)),
                 pl.BlockSpec((1, tn), lambda i, j: (0, j)),
                 pl.BlockSpec((1, tn), lambda i, j: (0, j + nj))]
        n_w = 2
        if mode == "glu":
            out_shape = jax.ShapeDtypeStruct((t, half), F32)
            out_specs = lo
        else:
            out_shape = (jax.ShapeDtypeStruct((t, half), BF16), jax.ShapeDtypeStruct((t, half), F32))
            out_specs = (lo, lo)
    blocks = (_nbytes((ahead.rows, d), F32) + _nbytes(mod.shape[2:], F32)
              + n_w * _nbytes((d, tn), w.dtype) + n_w * _nbytes((tm, tn), F32))
    return pl.pallas_call(
        functools.partial(_inproj_kernel, mode=mode, has_bias=b is not None, tm=tm, ahead=ahead, lay=lay),
        grid=(t // tm, nj),
        in_specs=specs,
        out_specs=out_specs,
        out_shape=out_shape,
        scratch_shapes=[pltpu.VMEM((tm, d), BF16), pltpu.VMEM((tm, d), BF16)],
        compiler_params=pltpu.CompilerParams(
            dimension_semantics=("arbitrary", "arbitrary"),
            vmem_limit_bytes=_vmem_limit(blocks, _nbytes((2, tm, d), BF16),
                                         n_w * (3 * _nbytes((tm, tn), F32) + _nbytes((d, tn), BF16)))),
        name="inproj_" + mode,
    )(*ins)


def _outproj_kernel(x_ref, a_ref, mod_ref, w_ref, *refs, has_bias, tm, nk, lay):
    i = pl.program_id(0)
    k = pl.program_id(1)
    b_ref, o_ref = (refs[0], refs[1]) if has_bias else (None, refs[0])
    groups = tm // CHUNK

    def gates():
        for grp in range(groups):
            yield slice(grp * CHUNK, (grp + 1) * CHUNK), _gate_row(mod_ref, i * tm + grp * CHUNK, lay)

    p = _dot(a_ref[...], w_ref[...])
    if nk == 1:
        for rows, gate in gates():
            acc = p[rows, :] if b_ref is None else p[rows, :] + b_ref[...]
            o_ref[rows, :] = x_ref[rows, :] + gate * acc
    else:
        @pl.when(k == 0)
        def _():
            for rows, gate in gates():
                o_ref[rows, :] = x_ref[rows, :] if b_ref is None else x_ref[rows, :] + gate * b_ref[...]

        for rows, gate in gates():
            o_ref[rows, :] += gate * p[rows, :]


def _outproj(x, a, mod, layer, w, b, lay):
    t, d = x.shape
    kdim = a.shape[1]
    tm = _pick(t, (512, 384, 256, 128))
    tk = _pick(kdim, (2048, 1024, 512))
    nk = kdim // tk
    ins = [x, a, mod, w]
    specs = [pl.BlockSpec((tm, d), lambda i, k: (i, 0)),
             pl.BlockSpec((tm, tk), lambda i, k: (i, k)),
             _mod_spec(mod, layer, 1, 2),
             pl.BlockSpec((tk, d), lambda i, k: (k, 0))]
    if b is not None:
        ins.append(b.reshape(1, d))
        specs.append(pl.BlockSpec((1, d), lambda i, k: (0, 0)))
    blocks = (2 * _nbytes((tm, d), F32) + _nbytes((tm, tk), BF16) + _nbytes(mod.shape[2:], F32)
              + _nbytes((tk, d), BF16))
    return pl.pallas_call(
        functools.partial(_outproj_kernel, has_bias=b is not None, tm=tm, nk=nk, lay=lay),
        grid=(t // tm, nk),
        in_specs=specs,
        out_specs=pl.BlockSpec((tm, d), lambda i, k: (i, 0)),
        out_shape=jax.ShapeDtypeStruct((t, d), F32),
        compiler_params=pltpu.CompilerParams(
            dimension_semantics=("parallel", "arbitrary"),
            vmem_limit_bytes=_vmem_limit(blocks)),
        name="outproj",
    )(*ins)


def _fill_history(ext_ref, prev_ref, cache_ref, n, n_prompt_tiles, tiles_per_seq):
    rows = prev_ref.shape[0]
    is_sample = n >= n_prompt_tiles
    starts_seq = jnp.logical_and(n % tiles_per_seq == 0, jnp.logical_not(is_sample))

    @pl.when(is_sample)
    def _():
        ext_ref[0:rows, :] = cache_ref[...]

    @pl.when(starts_seq)
    def _():
        ext_ref[0:rows, :] = jnp.zeros(prev_ref.shape, ext_ref.dtype)

    @pl.when(jnp.logical_not(jnp.logical_or(is_sample, starts_seq)))
    def _():
        ext_ref[0:rows, :] = prev_ref[...]


def _history_specs(rows, d, n_prompt_tiles, sub=1, s=0):
    per_tile = CHUNK // rows
    return [pl.BlockSpec((None, rows, d), lambda i: (jnp.maximum((i * sub + s) * per_tile - 1, 0), 0, 0)),
            pl.BlockSpec((None, rows, d), lambda i: (jnp.maximum(i * sub + s - n_prompt_tiles, 0), 0, 0))]


def _pad_cache(cache, rows):
    return jnp.pad(cache, ((0, 0), (rows - cache.shape[1], 0), (0, 0)))


def _conv_mid_kernel(g_ref, prev_ref, cache_ref, wdw_ref, bdw_ref, lng_ref, lnb_ref, o_ref, ext_ref, y_ref, *,
                     tt, cb, n_prompt_tiles, tiles_per_seq):
    d = g_ref.shape[-1]
    _fill_history(ext_ref, prev_ref, cache_ref, pl.program_id(0), n_prompt_tiles, tiles_per_seq)
    ext_ref[CONV_HALO:CONV_HALO + tt, :] = g_ref[...]
    ext_ref[CONV_HALO + tt:CONV_HALO + tt + SUBLANES, :] = jnp.zeros((SUBLANES, d), F32)
    first = CONV_HALO - CONV_STATE
    for c0 in range(0, d, cb):
        cols = slice(c0, c0 + cb)
        y = None
        for r in range(SUBLANES):
            u = None
            for k in range(CONV_WIDTH):
                if (first + k) % SUBLANES != r:
                    continue
                a0 = (first + k) - r
                term = ext_ref[a0:a0 + tt + SUBLANES, cols] * wdw_ref[k:k + 1, cols]
                u = term if u is None else u + term
            if u is None:
                continue
            piece = u[r:r + tt, :]
            y = piece if y is None else y + piece
        y_ref[:, cols] = y + bdw_ref[:, cols]
    y = y_ref[...]
    yc = y - jnp.mean(y, axis=-1, keepdims=True)
    var = jnp.mean(yc * yc, axis=-1, keepdims=True)
    z = yc * lax.rsqrt(var + EPS) * lng_ref[...] + lnb_ref[...]
    o_ref[...] = (z * jax.nn.sigmoid(z)).astype(o_ref.dtype)


def _conv_mid(g, cache, w_dw, b_dw, ln_g, ln_b, lay):
    t, d = g.shape
    tt = CHUNK
    n = t // tt
    n_prompt_tiles = lay.tp // tt
    cb = 2 * LANES
    blocks = (_nbytes((tt, d), F32) + 2 * _nbytes((CONV_HALO, d), F32) + _nbytes((32, d), F32)
              + 3 * _nbytes((SUBLANES, d), F32) + _nbytes((tt, d), BF16))
    vec = pl.BlockSpec((1, d), lambda i: (0, 0))
    ext_rows = CONV_HALO + tt + SUBLANES
    return pl.pallas_call(
        functools.partial(_conv_mid_kernel, tt=tt, cb=cb, n_prompt_tiles=n_prompt_tiles,
                          tiles_per_seq=lay.sp // tt),
        grid=(n,),
        in_specs=[pl.BlockSpec((None, tt, d), lambda i: (i, 0, 0))]
        + _history_specs(CONV_HALO, d, n_prompt_tiles) + [
            pl.BlockSpec((CONV_WIDTH, d), lambda i: (0, 0)),
            vec, vec, vec,
        ],
        out_specs=pl.BlockSpec((None, tt, d), lambda i: (i, 0, 0)),
        out_shape=jax.ShapeDtypeStruct((n, tt, d), BF16),
        scratch_shapes=[pltpu.VMEM((ext_rows, d), F32), pltpu.VMEM((tt, d), F32)],
        compiler_params=pltpu.CompilerParams(
            dimension_semantics=("parallel",),
            vmem_limit_bytes=_vmem_limit(blocks, _nbytes((ext_rows + tt, d), F32), 4 * _nbytes((tt, d), F32))),
        name="conv_mid",
    )(g.reshape(n, tt, d), g.reshape(t // CONV_HALO, CONV_HALO, d), _pad_cache(cache, CONV_HALO),
      w_dw, b_dw.reshape(1, d), ln_g.reshape(1, d), ln_b.reshape(1, d))


def _pool_mid_kernel(p_ref, *refs, sub, n_prompt_tiles, tiles_per_seq):
    hist_refs = refs[:2 * sub]
    wg_ref, sc_ref, o_ref, ext_ref, pooled_ref = refs[2 * sub:]
    n = pl.program_id(0)
    d = p_ref.shape[-1]
    gw = d // len(POOL_WINDOWS)
    for s in range(sub):
        tile = n * sub + s
        rows = slice(s * CHUNK, (s + 1) * CHUNK)
        _fill_history(ext_ref, hist_refs[2 * s], hist_refs[2 * s + 1], tile, n_prompt_tiles, tiles_per_seq)
        ext_ref[POOL_HALO:POOL_HALO + CHUNK, :] = p_ref[rows, :]
        pos0 = jnp.where(tile < n_prompt_tiles, (tile % tiles_per_seq) * CHUNK, PAST_LEN)
        pos = pos0 + lax.broadcasted_iota(jnp.int32, (CHUNK, 1), 0)
        for gi, w in enumerate(POOL_WINDOWS):
            cols = slice(gi * gw, (gi + 1) * gw)
            acc = ext_ref[POOL_HALO:POOL_HALO + CHUNK, cols]
            for back in range(1, w):
                acc = acc + ext_ref[POOL_HALO - back:POOL_HALO - back + CHUNK, cols]
            inv_cnt = 1.0 / jnp.minimum(pos + 1, w).astype(F32)
            pooled_ref[rows, cols] = (acc * inv_cnt - p_ref[rows, cols]).astype(pooled_ref.dtype)
    for gi in range(len(POOL_WINDOWS)):
        cols = slice(gi * gw, (gi + 1) * gw)
        z = _dot(pooled_ref[:, cols], wg_ref[gi]) * sc_ref[:, cols]
        o_ref[:, cols] = z.astype(o_ref.dtype)


def _pool_mid(p, cache, w_grp, scale, lay):
    t, d = p.shape
    sub = 4
    tt = sub * CHUNK
    assert lay.tp % tt == 0 and lay.tsamp % tt == 0
    n_prompt_tiles = lay.tp // CHUNK
    hist = []
    hist_in = []
    for s in range(sub):
        hist += _history_specs(POOL_HALO, d, n_prompt_tiles, sub, s)
        hist_in += [p.reshape(t // POOL_HALO, POOL_HALO, d), _pad_cache(cache, POOL_HALO)]
    blocks = (_nbytes((tt, d), F32) + 2 * sub * _nbytes((POOL_HALO, d), F32) + _nbytes(w_grp.shape, BF16)
              + _nbytes((SUBLANES, d), F32) + _nbytes((tt, d), BF16))
    return pl.pallas_call(
        functools.partial(_pool_mid_kernel, sub=sub, n_prompt_tiles=n_prompt_tiles,
                          tiles_per_seq=lay.sp // CHUNK),
        grid=(t // tt,),
        in_specs=[pl.BlockSpec((tt, d), lambda i: (i, 0))] + hist + [
            pl.BlockSpec(w_grp.shape, lambda i: (0, 0, 0)),
            pl.BlockSpec((1, d), lambda i: (0, 0)),
        ],
        out_specs=pl.BlockSpec((tt, d), lambda i: (i, 0)),
        out_shape=jax.ShapeDtypeStruct((t, d), BF16),
        scratch_shapes=[pltpu.VMEM((POOL_HALO + CHUNK, d), F32), pltpu.VMEM((tt, d), BF16)],
        compiler_params=pltpu.CompilerParams(
            dimension_semantics=("parallel",),
            vmem_limit_bytes=_vmem_limit(blocks, _nbytes((POOL_HALO + CHUNK, d), F32) + _nbytes((tt, d), BF16),
                                         4 * _nbytes((tt, d), F32))),
        name="pool_mid",
    )(p, *hist_in, w_grp, scale.reshape(1, d))


def _attn_kernel(q_ref, k0_ref, k1_ref, k2_ref, v0_ref, v1_ref, v2_ref, bias_ref, sink_ref, o_ref):
    keys = SWA_BAND + CHUNK
    pair_heads = Q_PER_KV // 2
    low_half = lax.broadcasted_iota(jnp.int32, (keys, LANES), 1) < HEAD_DIM
    tn_dims = (((1,), (1,)), ((), ()))
    nt_dims = (((0,), (0,)), ((), ()))

    units = []
    for m in range(N_KV_HEADS // 2):
        cols = slice(m * LANES, (m + 1) * LANES)
        kb = jnp.concatenate([k0_ref[:, cols], k1_ref[:, cols], k2_ref[:, cols]], axis=0) * ATTN_SCALE
        vb = jnp.concatenate([v0_ref[:, cols], v1_ref[:, cols], v2_ref[:, cols]], axis=0)
        kswap = pltpu.roll(kb, HEAD_DIM, axis=1)
        vswap = pltpu.roll(vb, HEAD_DIM, axis=1)
        for sub in range(2):
            kh = 2 * m + sub
            k_low, k_high = (kb, kswap) if sub == 0 else (kswap, kb)
            v_low, v_high = (vb, vswap) if sub == 0 else (vswap, vb)
            units.append((kh, 0, jnp.where(low_half, k_low, 0.0).astype(BF16),
                          jnp.where(low_half, v_low, 0.0).astype(BF16)))
            units.append((kh, 1, jnp.where(low_half, 0.0, k_high).astype(BF16),
                          jnp.where(low_half, 0.0, v_high).astype(BF16)))

    qstacks = []
    for kh in range(N_KV_HEADS):
        qstacks.append(jnp.concatenate(
            [q_ref[:, (kh * pair_heads + j) * LANES:(kh * pair_heads + j + 1) * LANES] for j in range(pair_heads)],
            axis=0).astype(BF16))

    scores = []
    for kh, par, kpad, _ in units:
        s = lax.dot_general(kpad, qstacks[kh], tn_dims, preferred_element_type=F32)
        scores.append(s + bias_ref[2 * kh + par])

    probs = []
    for (kh, par, _, _), s in zip(units, scores):
        sk = sink_ref[2 * kh + par]
        mx = jnp.maximum(jnp.max(s, axis=0, keepdims=True), sk)
        e = jnp.exp(s - mx)
        denom = jnp.sum(e, axis=0, keepdims=True) + jnp.exp(sk - mx)
        probs.append((e * (1.0 / denom)).astype(BF16))

    for kh in range(N_KV_HEADS):
        acc = None
        for (ukh, par, _, vpad), p in zip(units, probs):
            if ukh != kh:
                continue
            o = lax.dot_general(p, vpad, nt_dims, preferred_element_type=F32)
            acc = o if acc is None else acc + o
        for j in range(pair_heads):
            c0 = (kh * pair_heads + j) * LANES
            o_ref[:, c0:c0 + LANES] = acc[j * CHUNK:(j + 1) * CHUNK, :].astype(o_ref.dtype)


def _attention(qkv, kpad, vpad, bias, sinks, lay):
    t = qkv.shape[0]
    hw = N_HEADS * HEAD_DIM
    kvw = N_KV_HEADS * HEAD_DIM
    n_hist = SWA_BAND // CHUNK
    cps = lay.sp // CHUNK
    n_prompt_chunks = lay.bp * cps

    def kv_map(off):
        def index(c):
            base = jnp.where(c < n_prompt_chunks,
                             (c // cps) * (cps + n_hist) + c % cps,
                             lay.bp * (cps + n_hist) + (c - n_prompt_chunks) * (n_hist + 1))
            return (base + off, 0)
        return index

    def bias_map(c):
        return (jnp.where(c < n_prompt_chunks, jnp.minimum(c % cps, n_hist), n_hist), 0, 0, 0)

    kv_specs = [pl.BlockSpec((CHUNK, kvw), kv_map(o)) for o in range(n_hist + 1)]
    blocks = (_nbytes((CHUNK, hw), F32) + 6 * _nbytes((CHUNK, kvw), F32) + _nbytes(bias.shape[1:], F32)
              + _nbytes((2 * N_KV_HEADS, SUBLANES, bias.shape[-1]), F32) + _nbytes((CHUNK, hw), BF16))
    return pl.pallas_call(
        _attn_kernel,
        grid=(t // CHUNK,),
        in_specs=[pl.BlockSpec((CHUNK, hw), lambda c: (c, 0))] + kv_specs + kv_specs + [
            pl.BlockSpec((None,) + bias.shape[1:], bias_map),
            pl.BlockSpec(sinks.shape, lambda c: (0, 0, 0)),
        ],
        out_specs=pl.BlockSpec((CHUNK, hw), lambda c: (c, 0)),
        out_shape=jax.ShapeDtypeStruct((t, hw), BF16),
        compiler_params=pltpu.CompilerParams(
            dimension_semantics=("parallel",),
            vmem_limit_bytes=_vmem_limit(blocks, 0, 6 * _nbytes(bias.shape[1:], F32))),
        name="swa_attention",
    )(qkv, kpad, kpad, kpad, vpad, vpad, vpad, bias, sinks)


def _alibi_tables(swa_sinks):
    n_hist = SWA_BAND // CHUNK
    keys = SWA_BAND + CHUNK
    pair_heads = Q_PER_KV // 2

    def by_unit(per_head):
        return per_head.reshape(N_KV_HEADS, pair_heads, 2).transpose(0, 2, 1).reshape(2 * N_KV_HEADS, pair_heads)

    slopes = by_unit(jnp.exp2(-8.0 * jnp.arange(1, N_HEADS + 1, dtype=F32) / N_HEADS))
    dist = jnp.abs(jnp.arange(CHUNK)[None, :] + SWA_BAND - jnp.arange(keys)[:, None]).astype(F32)
    bias = (-slopes[:, None, :, None] * dist[None, :, None, :]).reshape(2 * N_KV_HEADS, keys, pair_heads * CHUNK)
    key_idx = jnp.arange(keys)[None, :, None]
    bias = jnp.stack([jnp.where(key_idx >= (n_hist - v) * CHUNK, bias, -jnp.inf) for v in range(n_hist + 1)])
    sinks = jnp.broadcast_to(by_unit(swa_sinks.astype(F32))[:, None, :, None],
                             (2 * N_KV_HEADS, 1, pair_heads, CHUNK)).reshape(2 * N_KV_HEADS, 1, pair_heads * CHUNK)
    return bias, sinks


def _gmlp_mid_kernel(u_ref, v_ref, lng_ref, lnb_ref, ws_ref, bs_ref, o_ref, vn_ref, *, n_prompt_blocks):
    n = pl.program_id(0)
    is_sample = n >= n_prompt_blocks
    v = v_ref[...]
    vc = v - jnp.mean(v, axis=-1, keepdims=True)
    var = jnp.mean(vc * vc, axis=-1, keepdims=True)
    vn = vc * lax.rsqrt(var + EPS) * lng_ref[...] + lnb_ref[...]
    vn_ref[...] = vn
    vnb = vn.astype(BF16)
    ll = GMLP_CHUNK
    rb = lax.broadcasted_iota(jnp.int32, (ll, ll), 0) // CHUNK
    cb = lax.broadcasted_iota(jnp.int32, (ll, ll), 1) // CHUNK
    back = rb - cb
    mask = (back >= 0) & (back <= jnp.where(is_sample, 0, 1))
    gw = u_ref.shape[-1] // GMLP_GROUPS
    for g in range(GMLP_GROUPS):
        w = jnp.where(mask, ws_ref[g], 0.0).astype(BF16)
        mixed = _dot(w, vnb[:, g * gw:(g + 1) * gw]) + bs_ref[g]
        o_ref[:, g * gw:(g + 1) * gw] = (u_ref[:, g * gw:(g + 1) * gw].astype(F32) * mixed).astype(o_ref.dtype)


def _gmlp_mid(u, v, ln_g, ln_b, ws2, bs2, n_prompt_blocks):
    t, gwid = u.shape
    ll = GMLP_CHUNK
    sel = lambda n: jnp.where(n >= n_prompt_blocks, 1, 0)
    blocks = (_nbytes((ll, gwid), BF16) + 2 * _nbytes((ll, gwid), F32) + 2 * _nbytes((SUBLANES, gwid), F32)
              + _nbytes((GMLP_GROUPS, ll, ll), F32) + _nbytes((GMLP_GROUPS, ll, LANES), F32)
              + _nbytes((ll, gwid), BF16))
    return pl.pallas_call(
        functools.partial(_gmlp_mid_kernel, n_prompt_blocks=n_prompt_blocks),
        grid=(t // ll,),
        in_specs=[
            pl.BlockSpec((ll, gwid), lambda n: (n, 0)),
            pl.BlockSpec((ll, gwid), lambda n: (n, 0)),
            pl.BlockSpec((1, gwid), lambda n: (0, 0)),
            pl.BlockSpec((1, gwid), lambda n: (0, 0)),
            pl.BlockSpec((None, GMLP_GROUPS, ll, ll), lambda n: (sel(n), 0, 0, 0)),
            pl.BlockSpec((None, GMLP_GROUPS, ll, 1), lambda n: (sel(n), 0, 0, 0)),
        ],
        out_specs=(pl.BlockSpec((ll, gwid), lambda n: (n, 0)),
                   pl.BlockSpec((ll, gwid), lambda n: (jnp.maximum(n - n_prompt_blocks, 0), 0))),
        out_shape=(jax.ShapeDtypeStruct((t, gwid), BF16),
                   jax.ShapeDtypeStruct((t - n_prompt_blocks * ll, gwid), F32)),
        compiler_params=pltpu.CompilerParams(
            dimension_semantics=("arbitrary",),
            vmem_limit_bytes=_vmem_limit(blocks, 0, 6 * _nbytes((ll, gwid), F32))),
        name="gmlp_mid",
    )(u, v, ln_g.reshape(1, gwid), ln_b.reshape(1, gwid), ws2, bs2)


def _final_norm_kernel(x_ref, g_ref, op_ref, os_ref, *, n_prompt_tiles):
    x = x_ref[...]
    ms = jnp.mean(x * x, axis=-1, keepdims=True)
    y = x * lax.rsqrt(ms + EPS) * g_ref[...]
    os_ref[...] = y

    @pl.when(pl.program_id(0) < n_prompt_tiles)
    def _():
        op_ref[...] = y


def _final_norm(x, g, lay):
    t, d = x.shape
    tm = _pick(lay.tp, (256, 128))
    assert lay.tsamp % tm == 0
    npt = lay.tp // tm
    return pl.pallas_call(
        functools.partial(_final_norm_kernel, n_prompt_tiles=npt),
        grid=(t // tm,),
        in_specs=[pl.BlockSpec((tm, d), lambda i: (i, 0)), pl.BlockSpec((1, d), lambda i: (0, 0))],
        out_specs=(pl.BlockSpec((tm, d), lambda i: (jnp.minimum(i, npt - 1), 0)),
                   pl.BlockSpec((tm, d), lambda i: (jnp.maximum(i - npt, 0), 0))),
        out_shape=(jax.ShapeDtypeStruct((lay.tp, d), F32), jax.ShapeDtypeStruct((lay.tsamp, d), F32)),
        compiler_params=pltpu.CompilerParams(
            dimension_semantics=("arbitrary",),
            vmem_limit_bytes=_vmem_limit(3 * _nbytes((tm, d), F32), 0, 2 * _nbytes((tm, d), F32))),
        name="final_norm",
    )(x, g.reshape(1, d))


def _state_tail(y, cache, keep, lay):
    d = y.shape[-1]
    tails = jnp.stack([y[(b + 1) * lay.sp - keep:(b + 1) * lay.sp] for b in range(lay.bp)])
    ys = y[lay.tp:].reshape(lay.bs, lay.ts, d)
    return tails, jnp.concatenate([cache, ys], axis=1)[:, -keep:]


def kernel(x_prompt, x_sample, c_prompt, c_sample, cache_conv, cache_pool, cache_swa_k, cache_swa_v, ada_w, ada_b, norm_g, ffn_w1, ffn_w3, ffn_w2, conv_w_pw1, conv_b_pw1, conv_w_dw, conv_b_dw, conv_ln_g, conv_ln_b, conv_w_pw2, conv_b_pw2, pool_w_in, pool_w_grp, pool_scale, pool_w_out, swa_wq, swa_wk, swa_wv, swa_wo, swa_sinks, gmlp_w_in, gmlp_b_in, gmlp_ln_g, gmlp_ln_b, gmlp_w_s, gmlp_b_s, gmlp_w_out, gmlp_b_out, final_g):
    bp, sp, d = x_prompt.shape
    bs, ts, _ = x_sample.shape
    lay = Layout(bp, sp, bs, ts)
    depth = ada_w.shape[0]
    kvw = N_KV_HEADS * HEAD_DIM
    hw = N_HEADS * HEAD_DIM

    x = jnp.concatenate([x_prompt.reshape(lay.tp, d), x_sample.reshape(lay.tsamp, d)], axis=0)
    c_all = jnp.pad(jnp.concatenate([c_prompt, c_sample], axis=0), ((0, SEQ_PAD - bp - bs), (0, 0)))
    mod = _ada_table(c_all, ada_w, ada_b).reshape(depth, 3, 3, SEQ_PAD, d)
    norm_g4 = norm_g.reshape(depth, 3, 1, d)

    w1b, w3b, w2b = ffn_w1, ffn_w3, ffn_w2

    outs = {}
    for i in range(depth):
        x = _ffn(x, mod, norm_g4, w1b, w3b, w2b, i, 0, 0, lay)
        kind = i % 4
        if kind == 0:
            g = _inproj(x, mod, norm_g4, i, conv_w_pw1, conv_b_pw1, lay, "glu")
            a = _conv_mid(g, cache_conv, conv_w_dw, conv_b_dw, conv_ln_g, conv_ln_b, lay)
            x = _outproj(x, a.reshape(lay.t, d), mod, i, conv_w_pw2.astype(BF16), conv_b_pw2, lay)
            outs["conv"] = _state_tail(g, cache_conv, CONV_STATE, lay)
        elif kind == 1:
            p = _inproj(x, mod, norm_g4, i, pool_w_in, None, lay, "single")
            a = _pool_mid(p, cache_pool, pool_w_grp.astype(BF16), pool_scale, lay)
            x = _outproj(x, a.reshape(lay.t, d), mod, i, pool_w_out.astype(BF16), None, lay)
            outs["pool"] = _state_tail(p, cache_pool, POOL_STATE, lay)
        elif kind == 2:
            w_qkv = jnp.concatenate([swa_wq, swa_wk, swa_wv], axis=1)
            qkv = _inproj(x, mod, norm_g4, i, w_qkv, None, lay, "single")
            r = cache_swa_k.shape[1]
            assert r == SWA_BAND

            def padded(col0, cache):
                new = qkv[:, col0:col0 + kvw]
                newp = jnp.pad(new[:lay.tp].reshape(bp, sp, kvw), ((0, 0), (SWA_BAND, 0), (0, 0)))
                news = jnp.concatenate([cache.reshape(bs, r, kvw), new[lay.tp:].reshape(bs, ts, kvw)], axis=1)
                tail_p = new[:lay.tp].reshape(bp, sp, kvw)[:, sp - SWA_BAND:]
                tail_s = news[:, ts:]
                both = jnp.concatenate([newp.reshape(-1, kvw), news.reshape(-1, kvw)], axis=0)
                return both, tail_p.reshape(bp, SWA_BAND, N_KV_HEADS, HEAD_DIM), tail_s.reshape(bs, r, N_KV_HEADS, HEAD_DIM)

            kpad, k_p, k_s = padded(hw, cache_swa_k)
            vpad, v_p, v_s = padded(hw + kvw, cache_swa_v)
            bias, sinks = _alibi_tables(swa_sinks)
            a = _attention(qkv, kpad, vpad, bias, sinks, lay)
            x = _outproj(x, a, mod, i, swa_wo.astype(BF16), None, lay)
            outs["kv"] = (k_p, v_p, k_s, v_s)
        else:
            u, v = _inproj(x, mod, norm_g4, i, gmlp_w_in, gmlp_b_in, lay, "gelu2")
            ll = GMLP_CHUNK
            ws2 = jnp.stack([gmlp_w_s[:, :ll, :ll], jnp.tile(gmlp_w_s[:, :ts, :ts], (1, ll // ts, ll // ts))])
            bs2 = jnp.stack([gmlp_b_s[:, :ll], jnp.tile(gmlp_b_s[:, :ts], (1, ll // ts))])[..., None]
            a, vn = _gmlp_mid(u, v, gmlp_ln_g, gmlp_ln_b, ws2, bs2, lay.tp // ll)
            x = _outproj(x, a, mod, i, gmlp_w_out.astype(BF16), gmlp_b_out, lay)
            outs["gmlp_v"] = vn.reshape(bs, ts, -1)
        x = _ffn(x, mod, norm_g4, w1b, w3b, w2b, i, 2, 1, lay)

    y_p, y_s = _final_norm(x, final_g, lay)
    y_prompt = y_p.reshape(bp, sp, d)
    y_sample = y_s.reshape(bs, ts, d)
    conv_p, conv_s = outs["conv"]
    pool_p, pool_s = outs["pool"]
    k_p, v_p, k_s, v_s = outs["kv"]
    return (y_prompt, y_sample, conv_p, conv_s, pool_p, pool_s, k_p, v_p, k_s, v_s, outs["gmlp_v"])
```

```python
import functools

import jax
import jax.numpy as jnp
from jax import lax
from jax.experimental import pallas as pl
from jax.experimental.pallas import tpu as pltpu

F32 = jnp.float32
BF16 = jnp.bfloat16

EPS = 1e-6
MACARON = 0.5
N_MOD = 9
CONV_WIDTH = 31
CONV_STATE = CONV_WIDTH - 1
POOL_WINDOWS = (2, 4, 8, 16)
POOL_STATE = max(POOL_WINDOWS) - 1
HEAD_DIM = 64
N_KV_HEADS = 4
Q_PER_KV = 8
N_HEADS = N_KV_HEADS * Q_PER_KV
CHUNK = 64
SWA_BAND = 128
ATTN_SCALE = HEAD_DIM ** -0.5
GMLP_CHUNK = 128
GMLP_GROUPS = 8
PAST_LEN = 1024

LANES = 128
SUBLANES = 8
ROW_CHUNK = 16
CONV_HALO = 32
POOL_HALO = 16
SEQ_PAD = 32

V7X_VMEM_LIMIT_CAP = 60 * 1024 * 1024


class Layout:
    def __init__(self, bp, sp, bs, ts):
        self.bp, self.sp, self.bs, self.ts = bp, sp, bs, ts
        self.tp = bp * sp
        self.tsamp = bs * ts
        self.t = self.tp + self.tsamp
        assert sp % GMLP_CHUNK == 0 and ts == CHUNK and self.tsamp % GMLP_CHUNK == 0
        assert bp + bs <= SEQ_PAD

    def seq_of(self, row0):
        return jnp.where(row0 < self.tp, row0 // self.sp, self.bp + (row0 - self.tp) // self.ts)


def _pick(total, candidates):
    for c in candidates:
        if total % c == 0:
            return c
    raise ValueError(f"no tile for {total} in {candidates}")


def _vmem_limit(pipelined_bytes, scratch_bytes=0, temp_bytes=0):
    est = 2 * pipelined_bytes + scratch_bytes + temp_bytes + (4 << 20)
    return int(min(max(est, 16 << 20), V7X_VMEM_LIMIT_CAP))


def _nbytes(shape, dtype):
    n = 1
    for s in shape:
        n *= s
    return n * jnp.dtype(dtype).itemsize


def _dot(a, b):
    return jnp.dot(a, b, preferred_element_type=F32)


def _norm_mod_group(h_ref, h_row, x_ref, x_row, mod_ref, ng, global_row, lay):
    seq = lay.seq_of(global_row)
    shift = mod_ref[0, pl.ds(seq, 1), :]
    gain = ng * (1.0 + mod_ref[1, pl.ds(seq, 1), :])
    for s in range(CHUNK // ROW_CHUNK):
        xg = x_ref[pl.ds(x_row + s * ROW_CHUNK, ROW_CHUNK), :]
        ms = jnp.mean(xg * xg, axis=-1, keepdims=True)
        h_ref[pl.ds(h_row + s * ROW_CHUNK, ROW_CHUNK), :] = (
            xg * lax.rsqrt(ms + EPS) * gain + shift).astype(h_ref.dtype)


def _norm_mod_rows(h_ref, x_ref, mod_ref, ng_ref, row_base, tm, lay):
    ng = ng_ref[...]

    def body(c, carry):
        r = pl.multiple_of(c * CHUNK, CHUNK)
        _norm_mod_group(h_ref, r, x_ref, r, mod_ref, ng, row_base + r, lay)
        return carry

    lax.fori_loop(0, tm // CHUNK, body, 0)


class AheadPlan:
    def __init__(self, tm, n_tiles, n_steps):
        self.tm, self.n_tiles = tm, n_tiles
        groups = tm // CHUNK
        need = -(-groups // n_steps)
        self.per_step = next(g for g in range(need, groups + 1) if groups % g == 0)
        self.blocks = groups // self.per_step
        self.rows = self.per_step * CHUNK

    def spec(self, d):
        def index(i, j):
            nxt = jnp.minimum(i + 1, self.n_tiles - 1)
            return (nxt * self.blocks + jnp.minimum(j, self.blocks - 1), 0)
        return pl.BlockSpec((self.rows, d), index)

    def current_spec(self, d):
        return pl.BlockSpec((self.rows, d), lambda i, j: (i * self.blocks + jnp.minimum(j, self.blocks - 1), 0))

    def run(self, h_next, xa_ref, mod_ref, ng_ref, i, j, lay):
        ng = ng_ref[...]
        nxt = jnp.minimum(i + 1, self.n_tiles - 1)
        row0 = pl.multiple_of(jnp.minimum(j, self.blocks - 1) * self.rows, CHUNK)
        for g in range(self.per_step):
            _norm_mod_group(h_next, row0 + g * CHUNK, xa_ref, g * CHUNK, mod_ref, ng,
                            nxt * self.tm + row0 + g * CHUNK, lay)


def _by_parity(i, ha_ref, hb_ref, body):
    @pl.when(i % 2 == 0)
    def _():
        body(ha_ref, hb_ref)

    @pl.when(i % 2 == 1)
    def _():
        body(hb_ref, ha_ref)


def _gate_row(mod_ref, global_row, lay):
    return mod_ref[2, pl.ds(lay.seq_of(global_row), 1), :]


def _first_tile_kernel(x_ref, mod_ref, ng_ref, o_ref, *, rows, lay):
    _norm_mod_rows(o_ref, x_ref, mod_ref, ng_ref, pl.program_id(0) * rows, rows, lay)


def _first_tile_h(x, mod, norm_g4, layer, sub, tm, lay):
    d = x.shape[1]
    rows = _pick(tm, (512, 256, 128))
    return pl.pallas_call(
        functools.partial(_first_tile_kernel, rows=rows, lay=lay),
        grid=(tm // rows,),
        in_specs=[pl.BlockSpec((rows, d), lambda i: (i, 0)),
                  _mod_spec(mod, layer, sub, 1),
                  pl.BlockSpec((None, None, 1, d), lambda i: (layer, sub, 0, 0))],
        out_specs=pl.BlockSpec((rows, d), lambda i: (i, 0)),
        out_shape=jax.ShapeDtypeStruct((tm, d), BF16),
        compiler_params=pltpu.CompilerParams(
            dimension_semantics=("parallel",),
            vmem_limit_bytes=_vmem_limit(2 * _nbytes((rows, d), F32) + _nbytes(mod.shape[2:], F32))),
        name="first_tile_norm",
    )(x, mod, norm_g4)


def _mod_spec(mod, layer, sub, grid_rank):
    zeros = (0,) * 3
    block = (None, None) + mod.shape[2:]
    if grid_rank == 1:
        return pl.BlockSpec(block, lambda i: (layer, sub) + zeros)
    return pl.BlockSpec(block, lambda i, j: (layer, sub) + zeros)


def _norm_g_spec(norm_g4, layer, sub):
    return pl.BlockSpec((None, None, 1, norm_g4.shape[-1]), lambda i, j: (layer, sub, 0, 0))


def _ada_kernel(c_ref, w_ref, b_ref, o_ref):
    c = c_ref[...]
    cond = (c * jax.nn.sigmoid(c)).astype(BF16)
    o_ref[...] = _dot(cond, w_ref[...].astype(BF16)) + b_ref[...]


def _ada_table(c_all, ada_w, ada_b):
    depth, d, n = ada_w.shape
    n_mod = n // d
    blocks = 2 * _nbytes((SEQ_PAD, d), F32) + _nbytes((d, d), F32) + _nbytes((SUBLANES, d), F32)
    return pl.pallas_call(
        _ada_kernel,
        grid=(depth, n_mod),
        in_specs=[
            pl.BlockSpec((SEQ_PAD, d), lambda l, j: (0, 0)),
            pl.BlockSpec((None, d, d), lambda l, j: (l, 0, j)),
            pl.BlockSpec((None, None, 1, d), lambda l, j: (l, j, 0, 0)),
        ],
        out_specs=pl.BlockSpec((None, None, SEQ_PAD, d), lambda l, j: (l, j, 0, 0)),
        out_shape=jax.ShapeDtypeStruct((depth, n_mod, SEQ_PAD, d), F32),
        compiler_params=pltpu.CompilerParams(
            dimension_semantics=("parallel", "arbitrary"),
            vmem_limit_bytes=_vmem_limit(blocks, temp_bytes=_nbytes((d, d), BF16))),
        name="ada_table",
    )(c_all, ada_w, ada_b.reshape(depth, n_mod, 1, d))


def _ffn_kernel(h0_ref, xc_ref, xa_ref, mod_ref, ng_ref, w1_ref, w3_ref, w2_ref, o_ref, ha_ref, hb_ref, *,
                tm, sub_cols, ahead, lay):
    i = pl.program_id(0)
    j = pl.program_id(1)

    @pl.when(jnp.logical_and(i == 0, j == 0))
    def _():
        pltpu.sync_copy(h0_ref, ha_ref)

    @pl.when(j == 0)
    def _():
        o_ref[...] = jnp.zeros_like(o_ref)

    def step(h_ref, h_next):
        row0 = pl.multiple_of(jnp.minimum(j, ahead.blocks - 1) * ahead.rows, CHUNK)
        o_ref[pl.ds(row0, ahead.rows), :] += jnp.where(j < ahead.blocks, xc_ref[...], 0.0)
        h = h_ref[...]
        pieces = []
        for c0 in range(0, w1_ref.shape[-1], sub_cols):
            cols = slice(c0, c0 + sub_cols)
            a = _dot(h, w1_ref[:, cols].astype(BF16))
            b = _dot(h, w3_ref[:, cols].astype(BF16))
            pieces.append((a * jax.nn.sigmoid(a) * b).astype(BF16))
        g = pieces[0] if len(pieces) == 1 else jnp.concatenate(pieces, axis=1)
        p = _dot(g, w2_ref[...].astype(BF16))
        for grp in range(tm // CHUNK):
            rows = slice(grp * CHUNK, (grp + 1) * CHUNK)
            gate = MACARON * _gate_row(mod_ref, i * tm + grp * CHUNK, lay)
            o_ref[rows, :] += gate * p[rows, :]
        ahead.run(h_next, xa_ref, mod_ref, ng_ref, i, j, lay)

    _by_parity(i, ha_ref, hb_ref, step)


def _ffn(x, mod, norm_g4, w1, w3, w2, layer, sub, which, lay):
    t, d = x.shape
    f = w1.shape[-1]
    tm = _pick(t, (1024, 768, 512, 384, 256, 128))
    tf = _pick(f, (512, 256, 128))
    nj = f // tf
    ahead = AheadPlan(tm, t // tm, nj)
    h0 = _first_tile_h(x, mod, norm_g4, layer, sub, tm, lay)
    blocks = (_nbytes((tm, d), F32) + 2 * _nbytes((ahead.rows, d), F32) + _nbytes(mod.shape[2:], F32)
              + 3 * _nbytes((d, tf), w1.dtype))
    temps = 2 * _nbytes((tm, tf), F32) + _nbytes((tm, tf), BF16) + 3 * _nbytes((d, tf), BF16)
    return pl.pallas_call(
        functools.partial(_ffn_kernel, tm=tm, sub_cols=_pick(tf, (256, 128)), ahead=ahead, lay=lay),
        grid=(t // tm, nj),
        in_specs=[
            pl.BlockSpec(memory_space=pl.ANY),
            ahead.current_spec(d),
            ahead.spec(d),
            _mod_spec(mod, layer, sub, 2),
            _norm_g_spec(norm_g4, layer, sub),
            pl.BlockSpec((None, None, d, tf), lambda i, j: (layer, which, 0, j)),
            pl.BlockSpec((None, None, d, tf), lambda i, j: (layer, which, 0, j)),
            pl.BlockSpec((None, None, tf, d), lambda i, j: (layer, which, j, 0)),
        ],
        out_specs=pl.BlockSpec((tm, d), lambda i, j: (i, 0)),
        out_shape=jax.ShapeDtypeStruct((t, d), F32),
        scratch_shapes=[pltpu.VMEM((tm, d), BF16), pltpu.VMEM((tm, d), BF16)],
        compiler_params=pltpu.CompilerParams(
            dimension_semantics=("arbitrary", "arbitrary"),
            vmem_limit_bytes=_vmem_limit(blocks, _nbytes((2, tm, d), BF16), temps)),
        name="ffn",
    )(h0, x, x, mod, norm_g4, w1, w3, w2)


def _inproj_kernel(h0_ref, xa_ref, mod_ref, ng_ref, *refs, mode, has_bias, tm, sub_cols, ahead, lay):
    i = pl.program_id(0)
    j = pl.program_id(1)
    ha_ref, hb_ref = refs[-2:]

    @pl.when(jnp.logical_and(i == 0, j == 0))
    def _():
        pltpu.sync_copy(h0_ref, ha_ref)

    def step(h_ref, h_next):
        h = h_ref[...]
        if mode == "single":
            if has_bias:
                w_ref, b_ref, o_ref = refs[:3]
                o_ref[...] = _dot(h, w_ref[...].astype(BF16)) + b_ref[...]
            else:
                w_ref, o_ref = refs[:2]
                o_ref[...] = _dot(h, w_ref[...].astype(BF16))
        elif mode == "glu":
            wa_ref, wb_ref, ba_ref, bb_ref, o_ref = refs[:5]
            for c0 in range(0, o_ref.shape[-1], sub_cols):
                cols = slice(c0, c0 + sub_cols)
                a = _dot(h, wa_ref[:, cols].astype(BF16)) + ba_ref[:, cols]
                b = _dot(h, wb_ref[:, cols].astype(BF16)) + bb_ref[:, cols]
                o_ref[:, cols] = a * jax.nn.sigmoid(b)
        else:
            wa_ref, wb_ref, ba_ref, bb_ref, ou_ref, ov_ref = refs[:6]
            for c0 in range(0, ou_ref.shape[-1], sub_cols):
                cols = slice(c0, c0 + sub_cols)
                ou_ref[:, cols] = jax.nn.gelu(_dot(h, wa_ref[:, cols].astype(BF16)) + ba_ref[:, cols],
                                              approximate=True).astype(ou_ref.dtype)
                ov_ref[:, cols] = jax.nn.gelu(_dot(h, wb_ref[:, cols].astype(BF16)) + bb_ref[:, cols],
                                              approximate=True)
        ahead.run(h_next, xa_ref, mod_ref, ng_ref, i, j, lay)

    _by_parity(i, ha_ref, hb_ref, step)


def _inproj(x, mod, norm_g4, layer, w, b, lay, mode):
    t, d = x.shape
    n = w.shape[1]
    tm = _pick(t, (1536, 1024, 768, 512, 384, 256, 128))
    tn = _pick(n if mode == "single" else n // 2, (512, 256, 128))
    nj = (n if mode == "single" else n // 2) // tn
    ahead = AheadPlan(tm, t // tm, nj)
    h0 = _first_tile_h(x, mod, norm_g4, layer, 1, tm, lay)
    x_spec = pl.BlockSpec(memory_space=pl.ANY)
    mod_spec = _mod_spec(mod, layer, 1, 2)
    ng_spec = _norm_g_spec(norm_g4, layer, 1)
    lo = pl.BlockSpec((tm, tn), lambda i, j: (i, j))
    if mode == "single":
        ins = [h0, x, mod, norm_g4, w]
        specs = [x_spec, ahead.spec(d), mod_spec, ng_spec, pl.BlockSpec((d, tn), lambda i, j: (0, j))]
        if b is not None:
            ins.append(b.reshape(1, n))
            specs.append(pl.BlockSpec((1, tn), lambda i, j: (0, j)))
        out_shape = jax.ShapeDtypeStruct((t, n), F32)
        out_specs = lo
        n_w = 1
    else:
        half = n // 2
        ins = [h0, x, mod, norm_g4, w, w, b.reshape(1, n), b.reshape(1, n)]
        specs = [x_spec, ahead.spec(d), mod_spec, ng_spec,
                 pl.BlockSpec((d, tn), lambda i, j: (0, j)),
                 pl.BlockSpec((d, tn), lambda i, j: (0, j + nj)),
                 pl.BlockSpec((1, tn), lambda i, j: (0, j)),
                 pl.BlockSpec((1, tn), lambda i, j: (0, j + nj))]
        n_w = 2
        if mode == "glu":
            out_shape = jax.ShapeDtypeStruct((t, half), F32)
            out_specs = lo
        else:
            out_shape = (jax.ShapeDtypeStruct((t, half), BF16), jax.ShapeDtypeStruct((t, half), F32))
            out_specs = (lo, lo)
    blocks = (_nbytes((ahead.rows, d), F32) + _nbytes(mod.shape[2:], F32)
              + n_w * _nbytes((d, tn), w.dtype) + n_w * _nbytes((tm, tn), F32))
    return pl.pallas_call(
        functools.partial(_inproj_kernel, mode=mode, has_bias=b is not None, tm=tm,
                          sub_cols=_pick(tn, (256, 128)), ahead=ahead, lay=lay),
        grid=(t // tm, nj),
        in_specs=specs,
        out_specs=out_specs,
        out_shape=out_shape,
        scratch_shapes=[pltpu.VMEM((tm, d), BF16), pltpu.VMEM((tm, d), BF16)],
        compiler_params=pltpu.CompilerParams(
            dimension_semantics=("arbitrary", "arbitrary"),
            vmem_limit_bytes=_vmem_limit(blocks, _nbytes((2, tm, d), BF16),
                                         n_w * (3 * _nbytes((tm, tn), F32) + _nbytes((d, tn), BF16)))),
        name="inproj_" + mode,
    )(*ins)


def _outproj_kernel(x_ref, a_ref, mod_ref, w_ref, *refs, has_bias, tm, nk, lay):
    i = pl.program_id(0)
    k = pl.program_id(1)
    b_ref, o_ref = (refs[0], refs[1]) if has_bias else (None, refs[0])
    groups = tm // CHUNK

    def gates():
        for grp in range(groups):
            yield slice(grp * CHUNK, (grp + 1) * CHUNK), _gate_row(mod_ref, i * tm + grp * CHUNK, lay)

    p = _dot(a_ref[...], w_ref[...])
    if nk == 1:
        for rows, gate in gates():
            acc = p[rows, :] if b_ref is None else p[rows, :] + b_ref[...]
            o_ref[rows, :] = x_ref[rows, :] + gate * acc
    else:
        @pl.when(k == 0)
        def _():
            for rows, gate in gates():
                o_ref[rows, :] = x_ref[rows, :] if b_ref is None else x_ref[rows, :] + gate * b_ref[...]

        for rows, gate in gates():
            o_ref[rows, :] += gate * p[rows, :]


def _outproj(x, a, mod, layer, w, b, lay):
    t, d = x.shape
    kdim = a.shape[1]
    tm = _pick(t, (512, 384, 256, 128))
    tk = _pick(kdim, (2048, 1024, 512))
    nk = kdim // tk
    ins = [x, a, mod, w]
    specs = [pl.BlockSpec((tm, d), lambda i, k: (i, 0)),
             pl.BlockSpec((tm, tk), lambda i, k: (i, k)),
             _mod_spec(mod, layer, 1, 2),
             pl.BlockSpec((tk, d), lambda i, k: (k, 0))]
    if b is not None:
        ins.append(b.reshape(1, d))
        specs.append(pl.BlockSpec((1, d), lambda i, k: (0, 0)))
    blocks = (2 * _nbytes((tm, d), F32) + _nbytes((tm, tk), BF16) + _nbytes(mod.shape[2:], F32)
              + _nbytes((tk, d), BF16))
    return pl.pallas_call(
        functools.partial(_outproj_kernel, has_bias=b is not None, tm=tm, nk=nk, lay=lay),
        grid=(t // tm, nk),
        in_specs=specs,
        out_specs=pl.BlockSpec((tm, d), lambda i, k: (i, 0)),
        out_shape=jax.ShapeDtypeStruct((t, d), F32),
        compiler_params=pltpu.CompilerParams(
            dimension_semantics=("parallel", "arbitrary"),
            vmem_limit_bytes=_vmem_limit(blocks)),
        name="outproj",
    )(*ins)


def _fill_history(ext_ref, prev_ref, cache_ref, n, n_prompt_tiles, tiles_per_seq):
    rows = prev_ref.shape[0]
    is_sample = n >= n_prompt_tiles
    starts_seq = jnp.logical_and(n % tiles_per_seq == 0, jnp.logical_not(is_sample))

    @pl.when(is_sample)
    def _():
        ext_ref[0:rows, :] = cache_ref[...]

    @pl.when(starts_seq)
    def _():
        ext_ref[0:rows, :] = jnp.zeros(prev_ref.shape, ext_ref.dtype)

    @pl.when(jnp.logical_not(jnp.logical_or(is_sample, starts_seq)))
    def _():
        ext_ref[0:rows, :] = prev_ref[...]


def _history_specs(rows, d, n_prompt_tiles, sub=1, s=0):
    per_tile = CHUNK // rows
    return [pl.BlockSpec((None, rows, d), lambda i: (jnp.maximum((i * sub + s) * per_tile - 1, 0), 0, 0)),
            pl.BlockSpec((None, rows, d), lambda i: (jnp.maximum(i * sub + s - n_prompt_tiles, 0), 0, 0))]


def _pad_cache(cache, rows):
    return jnp.pad(cache, ((0, 0), (rows - cache.shape[1], 0), (0, 0)))


def _conv_mid_kernel(g_ref, prev_ref, cache_ref, wdw_ref, bdw_ref, lng_ref, lnb_ref, o_ref, ext_ref, y_ref, *,
                     tt, cb, n_prompt_tiles, tiles_per_seq):
    d = g_ref.shape[-1]
    _fill_history(ext_ref, prev_ref, cache_ref, pl.program_id(0), n_prompt_tiles, tiles_per_seq)
    ext_ref[CONV_HALO:CONV_HALO + tt, :] = g_ref[...]
    ext_ref[CONV_HALO + tt:CONV_HALO + tt + SUBLANES, :] = jnp.zeros((SUBLANES, d), F32)
    first = CONV_HALO - CONV_STATE
    for c0 in range(0, d, cb):
        cols = slice(c0, c0 + cb)
        y = None
        for r in range(SUBLANES):
            u = None
            for k in range(CONV_WIDTH):
                if (first + k) % SUBLANES != r:
                    continue
                a0 = (first + k) - r
                term = ext_ref[a0:a0 + tt + SUBLANES, cols] * wdw_ref[k:k + 1, cols]
                u = term if u is None else u + term
            if u is None:
                continue
            piece = u[r:r + tt, :]
            y = piece if y is None else y + piece
        y_ref[:, cols] = y + bdw_ref[:, cols]
    y = y_ref[...]
    yc = y - jnp.mean(y, axis=-1, keepdims=True)
    var = jnp.mean(yc * yc, axis=-1, keepdims=True)
    z = yc * lax.rsqrt(var + EPS) * lng_ref[...] + lnb_ref[...]
    o_ref[...] = (z * jax.nn.sigmoid(z)).astype(o_ref.dtype)


def _conv_mid(g, cache, w_dw, b_dw, ln_g, ln_b, lay):
    t, d = g.shape
    tt = CHUNK
    n = t // tt
    n_prompt_tiles = lay.tp // tt
    cb = 2 * LANES
    blocks = (_nbytes((tt, d), F32) + 2 * _nbytes((CONV_HALO, d), F32) + _nbytes((32, d), F32)
              + 3 * _nbytes((SUBLANES, d), F32) + _nbytes((tt, d), BF16))
    vec = pl.BlockSpec((1, d), lambda i: (0, 0))
    ext_rows = CONV_HALO + tt + SUBLANES
    return pl.pallas_call(
        functools.partial(_conv_mid_kernel, tt=tt, cb=cb, n_prompt_tiles=n_prompt_tiles,
                          tiles_per_seq=lay.sp // tt),
        grid=(n,),
        in_specs=[pl.BlockSpec((None, tt, d), lambda i: (i, 0, 0))]
        + _history_specs(CONV_HALO, d, n_prompt_tiles) + [
            pl.BlockSpec((CONV_WIDTH, d), lambda i: (0, 0)),
            vec, vec, vec,
        ],
        out_specs=pl.BlockSpec((None, tt, d), lambda i: (i, 0, 0)),
        out_shape=jax.ShapeDtypeStruct((n, tt, d), BF16),
        scratch_shapes=[pltpu.VMEM((ext_rows, d), F32), pltpu.VMEM((tt, d), F32)],
        compiler_params=pltpu.CompilerParams(
            dimension_semantics=("parallel",),
            vmem_limit_bytes=_vmem_limit(blocks, _nbytes((ext_rows + tt, d), F32), 4 * _nbytes((tt, d), F32))),
        name="conv_mid",
    )(g.reshape(n, tt, d), g.reshape(t // CONV_HALO, CONV_HALO, d), _pad_cache(cache, CONV_HALO),
      w_dw, b_dw.reshape(1, d), ln_g.reshape(1, d), ln_b.reshape(1, d))


def _pool_mid_kernel(p_ref, *refs, sub, n_prompt_tiles, tiles_per_seq):
    hist_refs = refs[:2 * sub]
    wg_ref, sc_ref, o_ref, ext_ref, pooled_ref = refs[2 * sub:]
    n = pl.program_id(0)
    d = p_ref.shape[-1]
    gw = d // len(POOL_WINDOWS)
    for s in range(sub):
        tile = n * sub + s
        rows = slice(s * CHUNK, (s + 1) * CHUNK)
        _fill_history(ext_ref, hist_refs[2 * s], hist_refs[2 * s + 1], tile, n_prompt_tiles, tiles_per_seq)
        ext_ref[POOL_HALO:POOL_HALO + CHUNK, :] = p_ref[rows, :]
        pos0 = jnp.where(tile < n_prompt_tiles, (tile % tiles_per_seq) * CHUNK, PAST_LEN)
        pos = pos0 + lax.broadcasted_iota(jnp.int32, (CHUNK, 1), 0)
        for gi, w in enumerate(POOL_WINDOWS):
            cols = slice(gi * gw, (gi + 1) * gw)
            acc = ext_ref[POOL_HALO:POOL_HALO + CHUNK, cols]
            for back in range(1, w):
                acc = acc + ext_ref[POOL_HALO - back:POOL_HALO - back + CHUNK, cols]
            inv_cnt = 1.0 / jnp.minimum(pos + 1, w).astype(F32)
            pooled_ref[rows, cols] = (acc * inv_cnt - p_ref[rows, cols]).astype(pooled_ref.dtype)
    for gi in range(len(POOL_WINDOWS)):
        cols = slice(gi * gw, (gi + 1) * gw)
        z = _dot(pooled_ref[:, cols], wg_ref[gi]) * sc_ref[:, cols]
        o_ref[:, cols] = z.astype(o_ref.dtype)


def _pool_mid(p, cache, w_grp, scale, lay):
    t, d = p.shape
    sub = 4
    tt = sub * CHUNK
    assert lay.tp % tt == 0 and lay.tsamp % tt == 0
    n_prompt_tiles = lay.tp // CHUNK
    hist = []
    hist_in = []
    for s in range(sub):
        hist += _history_specs(POOL_HALO, d, n_prompt_tiles, sub, s)
        hist_in += [p.reshape(t // POOL_HALO, POOL_HALO, d), _pad_cache(cache, POOL_HALO)]
    blocks = (_nbytes((tt, d), F32) + 2 * sub * _nbytes((POOL_HALO, d), F32) + _nbytes(w_grp.shape, BF16)
              + _nbytes((SUBLANES, d), F32) + _nbytes((tt, d), BF16))
    return pl.pallas_call(
        functools.partial(_pool_mid_kernel, sub=sub, n_prompt_tiles=n_prompt_tiles,
                          tiles_per_seq=lay.sp // CHUNK),
        grid=(t // tt,),
        in_specs=[pl.BlockSpec((tt, d), lambda i: (i, 0))] + hist + [
            pl.BlockSpec(w_grp.shape, lambda i: (0, 0, 0)),
            pl.BlockSpec((1, d), lambda i: (0, 0)),
        ],
        out_specs=pl.BlockSpec((tt, d), lambda i: (i, 0)),
        out_shape=jax.ShapeDtypeStruct((t, d), BF16),
        scratch_shapes=[pltpu.VMEM((POOL_HALO + CHUNK, d), F32), pltpu.VMEM((tt, d), BF16)],
        compiler_params=pltpu.CompilerParams(
            dimension_semantics=("parallel",),
            vmem_limit_bytes=_vmem_limit(blocks, _nbytes((POOL_HALO + CHUNK, d), F32) + _nbytes((tt, d), BF16),
                                         4 * _nbytes((tt, d), F32))),
        name="pool_mid",
    )(p, *hist_in, w_grp, scale.reshape(1, d))


def _attend_chunk(q_ref, k_refs, v_refs, bias_ref, sink_ref, o_ref, rows):
    k0_ref, k1_ref, k2_ref = k_refs
    v0_ref, v1_ref, v2_ref = v_refs
    keys = SWA_BAND + CHUNK
    pair_heads = Q_PER_KV // 2
    low_half = lax.broadcasted_iota(jnp.int32, (keys, LANES), 1) < HEAD_DIM
    tn_dims = (((1,), (1,)), ((), ()))
    nt_dims = (((0,), (0,)), ((), ()))

    units = []
    for m in range(N_KV_HEADS // 2):
        cols = slice(m * LANES, (m + 1) * LANES)
        kb = jnp.concatenate([k0_ref[:, cols], k1_ref[:, cols], k2_ref[:, cols]], axis=0) * ATTN_SCALE
        vb = jnp.concatenate([v0_ref[:, cols], v1_ref[:, cols], v2_ref[:, cols]], axis=0)
        kswap = pltpu.roll(kb, HEAD_DIM, axis=1)
        vswap = pltpu.roll(vb, HEAD_DIM, axis=1)
        for sub in range(2):
            kh = 2 * m + sub
            k_low, k_high = (kb, kswap) if sub == 0 else (kswap, kb)
            v_low, v_high = (vb, vswap) if sub == 0 else (vswap, vb)
            units.append((kh, 0, jnp.where(low_half, k_low, 0.0).astype(BF16),
                          jnp.where(low_half, v_low, 0.0).astype(BF16)))
            units.append((kh, 1, jnp.where(low_half, 0.0, k_high).astype(BF16),
                          jnp.where(low_half, 0.0, v_high).astype(BF16)))

    qstacks = []
    for kh in range(N_KV_HEADS):
        qstacks.append(jnp.concatenate(
            [q_ref[rows, (kh * pair_heads + j) * LANES:(kh * pair_heads + j + 1) * LANES] for j in range(pair_heads)],
            axis=0).astype(BF16))

    scores = []
    for kh, par, kpad, _ in units:
        s = lax.dot_general(kpad, qstacks[kh], tn_dims, preferred_element_type=F32)
        scores.append(s + bias_ref[2 * kh + par])

    probs = []
    for (kh, par, _, _), s in zip(units, scores):
        sk = sink_ref[2 * kh + par]
        mx = jnp.maximum(jnp.max(s, axis=0, keepdims=True), sk)
        e = jnp.exp(s - mx)
        denom = jnp.sum(e, axis=0, keepdims=True) + jnp.exp(sk - mx)
        probs.append((e * (1.0 / denom)).astype(BF16))

    for kh in range(N_KV_HEADS):
        acc = None
        for (ukh, par, _, vpad), p in zip(units, probs):
            if ukh != kh:
                continue
            o = lax.dot_general(p, vpad, nt_dims, preferred_element_type=F32)
            acc = o if acc is None else acc + o
        for j in range(pair_heads):
            c0 = (kh * pair_heads + j) * LANES
            o_ref[rows, c0:c0 + LANES] = acc[j * CHUNK:(j + 1) * CHUNK, :].astype(o_ref.dtype)


def _attn_kernel(q_ref, *refs, cpb):
    n_kv = SWA_BAND // CHUNK + 1
    k_refs = refs[:cpb * n_kv]
    v_refs = refs[cpb * n_kv:2 * cpb * n_kv]
    bias_refs = refs[2 * cpb * n_kv:2 * cpb * n_kv + cpb]
    sink_ref, o_ref = refs[2 * cpb * n_kv + cpb:]
    for ci in range(cpb):
        _attend_chunk(q_ref, k_refs[ci * n_kv:(ci + 1) * n_kv], v_refs[ci * n_kv:(ci + 1) * n_kv],
                      bias_refs[ci], sink_ref, o_ref, slice(ci * CHUNK, (ci + 1) * CHUNK))


def _attention(qkv, kpad, vpad, bias, sinks, lay):
    t = qkv.shape[0]
    hw = N_HEADS * HEAD_DIM
    kvw = N_KV_HEADS * HEAD_DIM
    n_hist = SWA_BAND // CHUNK
    cps = lay.sp // CHUNK
    n_prompt_chunks = lay.bp * cps

    cpb = 2 if (t // CHUNK) % 2 == 0 else 1

    def kv_map(ci, off):
        def index(step):
            c = step * cpb + ci
            base = jnp.where(c < n_prompt_chunks,
                             (c // cps) * (cps + n_hist) + c % cps,
                             lay.bp * (cps + n_hist) + (c - n_prompt_chunks) * (n_hist + 1))
            return (base + off, 0)
        return index

    def bias_map(ci):
        def index(step):
            c = step * cpb + ci
            return (jnp.where(c < n_prompt_chunks, jnp.minimum(c % cps, n_hist), n_hist), 0, 0, 0)
        return index

    kv_specs = [pl.BlockSpec((CHUNK, kvw), kv_map(ci, o)) for ci in range(cpb) for o in range(n_hist + 1)]
    bias_specs = [pl.BlockSpec((None,) + bias.shape[1:], bias_map(ci)) for ci in range(cpb)]
    rows = cpb * CHUNK
    blocks = (_nbytes((rows, hw), F32) + 2 * len(kv_specs) * _nbytes((CHUNK, kvw), F32)
              + cpb * _nbytes(bias.shape[1:], F32)
              + _nbytes((2 * N_KV_HEADS, SUBLANES, bias.shape[-1]), F32) + _nbytes((rows, hw), BF16))
    return pl.pallas_call(
        functools.partial(_attn_kernel, cpb=cpb),
        grid=(t // rows,),
        in_specs=[pl.BlockSpec((rows, hw), lambda c: (c, 0))] + kv_specs + kv_specs + bias_specs + [
            pl.BlockSpec(sinks.shape, lambda c: (0, 0, 0)),
        ],
        out_specs=pl.BlockSpec((rows, hw), lambda c: (c, 0)),
        out_shape=jax.ShapeDtypeStruct((t, hw), BF16),
        compiler_params=pltpu.CompilerParams(
            dimension_semantics=("parallel",),
            vmem_limit_bytes=_vmem_limit(blocks, 0, 6 * cpb * _nbytes(bias.shape[1:], F32))),
        name="swa_attention",
    )(qkv, *([kpad] * len(kv_specs)), *([vpad] * len(kv_specs)), *([bias] * cpb), sinks)


def _alibi_tables(swa_sinks):
    n_hist = SWA_BAND // CHUNK
    keys = SWA_BAND + CHUNK
    pair_heads = Q_PER_KV // 2

    def by_unit(per_head):
        return per_head.reshape(N_KV_HEADS, pair_heads, 2).transpose(0, 2, 1).reshape(2 * N_KV_HEADS, pair_heads)

    slopes = by_unit(jnp.exp2(-8.0 * jnp.arange(1, N_HEADS + 1, dtype=F32) / N_HEADS))
    dist = jnp.abs(jnp.arange(CHUNK)[None, :] + SWA_BAND - jnp.arange(keys)[:, None]).astype(F32)
    bias = (-slopes[:, None, :, None] * dist[None, :, None, :]).reshape(2 * N_KV_HEADS, keys, pair_heads * CHUNK)
    key_idx = jnp.arange(keys)[None, :, None]
    bias = jnp.stack([jnp.where(key_idx >= (n_hist - v) * CHUNK, bias, -jnp.inf) for v in range(n_hist + 1)])
    sinks = jnp.broadcast_to(by_unit(swa_sinks.astype(F32))[:, None, :, None],
                             (2 * N_KV_HEADS, 1, pair_heads, CHUNK)).reshape(2 * N_KV_HEADS, 1, pair_heads * CHUNK)
    return bias, sinks


def _gmlp_mid_kernel(u_ref, v_ref, lng_ref, lnb_ref, ws_ref, bs_ref, o_ref, vn_ref, *, n_prompt_blocks):
    n = pl.program_id(0)
    is_sample = n >= n_prompt_blocks
    v = v_ref[...]
    vc = v - jnp.mean(v, axis=-1, keepdims=True)
    var = jnp.mean(vc * vc, axis=-1, keepdims=True)
    vn = vc * lax.rsqrt(var + EPS) * lng_ref[...] + lnb_ref[...]
    vn_ref[...] = vn
    vnb = vn.astype(BF16)
    ll = GMLP_CHUNK
    rb = lax.broadcasted_iota(jnp.int32, (ll, ll), 0) // CHUNK
    cb = lax.broadcasted_iota(jnp.int32, (ll, ll), 1) // CHUNK
    back = rb - cb
    mask = (back >= 0) & (back <= jnp.where(is_sample, 0, 1))
    gw = u_ref.shape[-1] // GMLP_GROUPS
    for g in range(GMLP_GROUPS):
        w = jnp.where(mask, ws_ref[g], 0.0).astype(BF16)
        mixed = _dot(w, vnb[:, g * gw:(g + 1) * gw]) + bs_ref[g]
        o_ref[:, g * gw:(g + 1) * gw] = (u_ref[:, g * gw:(g + 1) * gw].astype(F32) * mixed).astype(o_ref.dtype)


def _gmlp_mid(u, v, ln_g, ln_b, ws2, bs2, n_prompt_blocks):
    t, gwid = u.shape
    ll = GMLP_CHUNK
    sel = lambda n: jnp.where(n >= n_prompt_blocks, 1, 0)
    blocks = (_nbytes((ll, gwid), BF16) + 2 * _nbytes((ll, gwid), F32) + 2 * _nbytes((SUBLANES, gwid), F32)
              + _nbytes((GMLP_GROUPS, ll, ll), F32) + _nbytes((GMLP_GROUPS, ll, LANES), F32)
              + _nbytes((ll, gwid), BF16))
    return pl.pallas_call(
        functools.partial(_gmlp_mid_kernel, n_prompt_blocks=n_prompt_blocks),
        grid=(t // ll,),
        in_specs=[
            pl.BlockSpec((ll, gwid), lambda n: (n, 0)),
            pl.BlockSpec((ll, gwid), lambda n: (n, 0)),
            pl.BlockSpec((1, gwid), lambda n: (0, 0)),
            pl.BlockSpec((1, gwid), lambda n: (0, 0)),
            pl.BlockSpec((None, GMLP_GROUPS, ll, ll), lambda n: (sel(n), 0, 0, 0)),
            pl.BlockSpec((None, GMLP_GROUPS, ll, 1), lambda n: (sel(n), 0, 0, 0)),
        ],
        out_specs=(pl.BlockSpec((ll, gwid), lambda n: (n, 0)),
                   pl.BlockSpec((ll, gwid), lambda n: (jnp.maximum(n - n_prompt_blocks, 0), 0))),
        out_shape=(jax.ShapeDtypeStruct((t, gwid), BF16),
                   jax.ShapeDtypeStruct((t - n_prompt_blocks * ll, gwid), F32)),
        compiler_params=pltpu.CompilerParams(
            dimension_semantics=("arbitrary",),
            vmem_limit_bytes=_vmem_limit(blocks, 0, 6 * _nbytes((ll, gwid), F32))),
        name="gmlp_mid",
    )(u, v, ln_g.reshape(1, gwid), ln_b.reshape(1, gwid), ws2, bs2)


def _final_norm_kernel(x_ref, g_ref, op_ref, os_ref, *, n_prompt_tiles):
    x = x_ref[...]
    ms = jnp.mean(x * x, axis=-1, keepdims=True)
    y = x * lax.rsqrt(ms + EPS) * g_ref[...]
    os_ref[...] = y

    @pl.when(pl.program_id(0) < n_prompt_tiles)
    def _():
        op_ref[...] = y


def _final_norm(x, g, lay):
    t, d = x.shape
    tm = _pick(lay.tp, (256, 128))
    assert lay.tsamp % tm == 0
    npt = lay.tp // tm
    return pl.pallas_call(
        functools.partial(_final_norm_kernel, n_prompt_tiles=npt),
        grid=(t // tm,),
        in_specs=[pl.BlockSpec((tm, d), lambda i: (i, 0)), pl.BlockSpec((1, d), lambda i: (0, 0))],
        out_specs=(pl.BlockSpec((tm, d), lambda i: (jnp.minimum(i, npt - 1), 0)),
                   pl.BlockSpec((tm, d), lambda i: (jnp.maximum(i - npt, 0), 0))),
        out_shape=(jax.ShapeDtypeStruct((lay.tp, d), F32), jax.ShapeDtypeStruct((lay.tsamp, d), F32)),
        compiler_params=pltpu.CompilerParams(
            dimension_semantics=("arbitrary",),
            vmem_limit_bytes=_vmem_limit(3 * _nbytes((tm, d), F32), 0, 2 * _nbytes((tm, d), F32))),
        name="final_norm",
    )(x, g.reshape(1, d))


def _state_tail(y, cache, keep, lay):
    d = y.shape[-1]
    tails = jnp.stack([y[(b + 1) * lay.sp - keep:(b + 1) * lay.sp] for b in range(lay.bp)])
    ys = y[lay.tp:].reshape(lay.bs, lay.ts, d)
    return tails, jnp.concatenate([cache, ys], axis=1)[:, -keep:]


def kernel(x_prompt, x_sample, c_prompt, c_sample, cache_conv, cache_pool, cache_swa_k, cache_swa_v, ada_w, ada_b, norm_g, ffn_w1, ffn_w3, ffn_w2, conv_w_pw1, conv_b_pw1, conv_w_dw, conv_b_dw, conv_ln_g, conv_ln_b, conv_w_pw2, conv_b_pw2, pool_w_in, pool_w_grp, pool_scale, pool_w_out, swa_wq, swa_wk, swa_wv, swa_wo, swa_sinks, gmlp_w_in, gmlp_b_in, gmlp_ln_g, gmlp_ln_b, gmlp_w_s, gmlp_b_s, gmlp_w_out, gmlp_b_out, final_g):
    bp, sp, d = x_prompt.shape
    bs, ts, _ = x_sample.shape
    lay = Layout(bp, sp, bs, ts)
    depth = ada_w.shape[0]
    kvw = N_KV_HEADS * HEAD_DIM
    hw = N_HEADS * HEAD_DIM

    x = jnp.concatenate([x_prompt.reshape(lay.tp, d), x_sample.reshape(lay.tsamp, d)], axis=0)
    c_all = jnp.pad(jnp.concatenate([c_prompt, c_sample], axis=0), ((0, SEQ_PAD - bp - bs), (0, 0)))
    mod = _ada_table(c_all, ada_w, ada_b).reshape(depth, 3, 3, SEQ_PAD, d)
    norm_g4 = norm_g.reshape(depth, 3, 1, d)

    w1b, w3b, w2b = ffn_w1, ffn_w3, ffn_w2

    outs = {}
    for i in range(depth):
        x = _ffn(x, mod, norm_g4, w1b, w3b, w2b, i, 0, 0, lay)
        kind = i % 4
        if kind == 0:
            g = _inproj(x, mod, norm_g4, i, conv_w_pw1, conv_b_pw1, lay, "glu")
            a = _conv_mid(g, cache_conv, conv_w_dw, conv_b_dw, conv_ln_g, conv_ln_b, lay)
            x = _outproj(x, a.reshape(lay.t, d), mod, i, conv_w_pw2.astype(BF16), conv_b_pw2, lay)
            outs["conv"] = _state_tail(g, cache_conv, CONV_STATE, lay)
        elif kind == 1:
            p = _inproj(x, mod, norm_g4, i, pool_w_in, None, lay, "single")
            a = _pool_mid(p, cache_pool, pool_w_grp.astype(BF16), pool_scale, lay)
            x = _outproj(x, a.reshape(lay.t, d), mod, i, pool_w_out.astype(BF16), None, lay)
            outs["pool"] = _state_tail(p, cache_pool, POOL_STATE, lay)
        elif kind == 2:
            w_qkv = jnp.concatenate([swa_wq, swa_wk, swa_wv], axis=1)
            qkv = _inproj(x, mod, norm_g4, i, w_qkv, None, lay, "single")
            r = cache_swa_k.shape[1]
            assert r == SWA_BAND

            def padded(col0, cache):
                new = qkv[:, col0:col0 + kvw]
                newp = jnp.pad(new[:lay.tp].reshape(bp, sp, kvw), ((0, 0), (SWA_BAND, 0), (0, 0)))
                news = jnp.concatenate([cache.reshape(bs, r, kvw), new[lay.tp:].reshape(bs, ts, kvw)], axis=1)
                tail_p = new[:lay.tp].reshape(bp, sp, kvw)[:, sp - SWA_BAND:]
                tail_s = news[:, ts:]
                both = jnp.concatenate([newp.reshape(-1, kvw), news.reshape(-1, kvw)], axis=0)
                return both, tail_p.reshape(bp, SWA_BAND, N_KV_HEADS, HEAD_DIM), tail_s.reshape(bs, r, N_KV_HEADS, HEAD_DIM)

            kpad, k_p, k_s = padded(hw, cache_swa_k)
            vpad, v_p, v_s = padded(hw + kvw, cache_swa_v)
            bias, sinks = _alibi_tables(swa_sinks)
            a = _attention(qkv, kpad, vpad, bias, sinks, lay)
            x = _outproj(x, a, mod, i, swa_wo.astype(BF16), None, lay)
            outs["kv"] = (k_p, v_p, k_s, v_s)
        else:
            u, v = _inproj(x, mod, norm_g4, i, gmlp_w_in, gmlp_b_in, lay, "gelu2")
            ll = GMLP_CHUNK
            ws2 = jnp.stack([gmlp_w_s[:, :ll, :ll], jnp.tile(gmlp_w_s[:, :ts, :ts], (1, ll // ts, ll // ts))])
            bs2 = jnp.stack([gmlp_b_s[:, :ll], jnp.tile(gmlp_b_s[:, :ts], (1, ll // ts))])[..., None]
            a, vn = _gmlp_mid(u, v, gmlp_ln_g, gmlp_ln_b, ws2, bs2, lay.tp // ll)
            x = _outproj(x, a, mod, i, gmlp_w_out.astype(BF16), gmlp_b_out, lay)
            outs["gmlp_v"] = vn.reshape(bs, ts, -1)
        x = _ffn(x, mod, norm_g4, w1b, w3b, w2b, i, 2, 1, lay)

    y_p, y_s = _final_norm(x, final_g, lay)
    y_prompt = y_p.reshape(bp, sp, d)
    y_sample = y_s.reshape(bs, ts, d)
    conv_p, conv_s = outs["conv"]
    pool_p, pool_s = outs["pool"]
    k_p, v_p, k_s, v_s = outs["kv"]
    return (y_prompt, y_sample, conv_p, conv_s, pool_p, pool_s, k_p, v_p, k_s, v_s, outs["gmlp_v"])
```

```python
import functools

import jax
import jax.numpy as jnp
from jax import lax
from jax.experimental import pallas as pl
from jax.experimental.pallas import tpu as pltpu

F32 = jnp.float32
BF16 = jnp.bfloat16

EPS = 1e-6
MACARON = 0.5
CONV_WIDTH = 31
CONV_STATE = CONV_WIDTH - 1
POOL_WINDOWS = (2, 4, 8, 16)
POOL_STATE = max(POOL_WINDOWS) - 1
HEAD_DIM = 64
N_KV_HEADS = 4
Q_PER_KV = 8
N_HEADS = N_KV_HEADS * Q_PER_KV
CHUNK = 64
SWA_BAND = 128
ATTN_SCALE = HEAD_DIM ** -0.5
GMLP_CHUNK = 128
GMLP_GROUPS = 8
PAST_LEN = 1024

LANES = 128
SUBLANES = 8
ROW_CHUNK = 16
CONV_HALO = 32
POOL_HALO = 16
SEQ_PAD = 32

V7X_VMEM_LIMIT_CAP = 60 * 1024 * 1024


class Layout:
    def __init__(self, bp, sp, bs, ts):
        self.bp, self.sp, self.bs, self.ts = bp, sp, bs, ts
        self.tp = bp * sp
        self.tsamp = bs * ts
        self.t = self.tp + self.tsamp
        assert sp % GMLP_CHUNK == 0 and ts == CHUNK and self.tsamp % GMLP_CHUNK == 0
        assert bp + bs <= SEQ_PAD

    def seq_of(self, row0):
        return jnp.where(row0 < self.tp, row0 // self.sp, self.bp + (row0 - self.tp) // self.ts)


def _pick(total, candidates):
    for c in candidates:
        if total % c == 0:
            return c
    raise ValueError(f"no tile for {total} in {candidates}")


def _vmem_limit(pipelined_bytes, scratch_bytes=0, temp_bytes=0):
    est = 2 * pipelined_bytes + scratch_bytes + temp_bytes + (4 << 20)
    return int(min(max(est, 16 << 20), V7X_VMEM_LIMIT_CAP))


def _nbytes(shape, dtype):
    n = 1
    for s in shape:
        n *= s
    return n * jnp.dtype(dtype).itemsize


def _dot(a, b):
    return jnp.dot(a, b, preferred_element_type=F32)


def _norm_mod_group(h_ref, h_row, x_ref, x_row, mod_ref, ng, global_row, lay):
    seq = lay.seq_of(global_row)
    shift = mod_ref[0, pl.ds(seq, 1), :]
    gain = ng * (1.0 + mod_ref[1, pl.ds(seq, 1), :])
    for s in range(CHUNK // ROW_CHUNK):
        xg = x_ref[pl.ds(x_row + s * ROW_CHUNK, ROW_CHUNK), :]
        ms = jnp.mean(xg * xg, axis=-1, keepdims=True)
        h_ref[pl.ds(h_row + s * ROW_CHUNK, ROW_CHUNK), :] = (
            xg * lax.rsqrt(ms + EPS) * gain + shift).astype(h_ref.dtype)


def _norm_mod_rows(h_ref, x_ref, mod_ref, ng_ref, row_base, tm, lay):
    ng = ng_ref[...]

    def body(c, carry):
        r = pl.multiple_of(c * CHUNK, CHUNK)
        _norm_mod_group(h_ref, r, x_ref, r, mod_ref, ng, row_base + r, lay)
        return carry

    lax.fori_loop(0, tm // CHUNK, body, 0)


class AheadPlan:
    def __init__(self, tm, n_tiles, n_steps):
        self.tm, self.n_tiles = tm, n_tiles
        groups = tm // CHUNK
        need = -(-groups // n_steps)
        self.per_step = next(g for g in range(need, groups + 1) if groups % g == 0)
        self.blocks = groups // self.per_step
        self.rows = self.per_step * CHUNK

    def spec(self, d):
        def index(i, j):
            nxt = jnp.minimum(i + 1, self.n_tiles - 1)
            return (nxt * self.blocks + jnp.minimum(j, self.blocks - 1), 0)
        return pl.BlockSpec((self.rows, d), index)

    def current_spec(self, d):
        return pl.BlockSpec((self.rows, d), lambda i, j: (i * self.blocks + jnp.minimum(j, self.blocks - 1), 0))

    def run(self, h_next, xa_ref, mod_ref, ng_ref, i, j, lay):
        ng = ng_ref[...]
        nxt = jnp.minimum(i + 1, self.n_tiles - 1)
        row0 = pl.multiple_of(jnp.minimum(j, self.blocks - 1) * self.rows, CHUNK)
        for g in range(self.per_step):
            _norm_mod_group(h_next, row0 + g * CHUNK, xa_ref, g * CHUNK, mod_ref, ng,
                            nxt * self.tm + row0 + g * CHUNK, lay)


def _by_parity(i, ha_ref, hb_ref, body):
    @pl.when(i % 2 == 0)
    def _():
        body(ha_ref, hb_ref)

    @pl.when(i % 2 == 1)
    def _():
        body(hb_ref, ha_ref)


def _gate_row(mod_ref, global_row, lay):
    return mod_ref[2, pl.ds(lay.seq_of(global_row), 1), :]


def _first_tile_kernel(x_ref, mod_ref, ng_ref, o_ref, *, rows, lay):
    _norm_mod_rows(o_ref, x_ref, mod_ref, ng_ref, pl.program_id(0) * rows, rows, lay)


def _first_tile_h(x, mod, norm_g4, layer, sub, tm, lay):
    d = x.shape[1]
    rows = _pick(tm, (512, 256, 128))
    return pl.pallas_call(
        functools.partial(_first_tile_kernel, rows=rows, lay=lay),
        grid=(tm // rows,),
        in_specs=[pl.BlockSpec((rows, d), lambda i: (i, 0)),
                  _mod_spec(mod, layer, sub, 1),
                  pl.BlockSpec((None, None, 1, d), lambda i: (layer, sub, 0, 0))],
        out_specs=pl.BlockSpec((rows, d), lambda i: (i, 0)),
        out_shape=jax.ShapeDtypeStruct((tm, d), BF16),
        compiler_params=pltpu.CompilerParams(
            dimension_semantics=("parallel",),
            vmem_limit_bytes=_vmem_limit(2 * _nbytes((rows, d), F32) + _nbytes(mod.shape[2:], F32))),
        name="first_tile_norm",
    )(x, mod, norm_g4)


def _mod_spec(mod, layer, sub, grid_rank):
    zeros = (0,) * 3
    block = (None, None) + mod.shape[2:]
    if grid_rank == 1:
        return pl.BlockSpec(block, lambda i: (layer, sub) + zeros)
    return pl.BlockSpec(block, lambda i, j: (layer, sub) + zeros)


def _norm_g_spec(norm_g4, layer, sub):
    return pl.BlockSpec((None, None, 1, norm_g4.shape[-1]), lambda i, j: (layer, sub, 0, 0))


def _ada_kernel(c_ref, w_ref, b_ref, o_ref):
    c = c_ref[...]
    cond = (c * jax.nn.sigmoid(c)).astype(BF16)
    o_ref[...] = _dot(cond, w_ref[...].astype(BF16)) + b_ref[...]


def _ada_table(c_all, ada_w, ada_b):
    depth, d, n = ada_w.shape
    n_mod = n // d
    blocks = 2 * _nbytes((SEQ_PAD, d), F32) + _nbytes((d, d), F32) + _nbytes((SUBLANES, d), F32)
    return pl.pallas_call(
        _ada_kernel,
        grid=(depth, n_mod),
        in_specs=[
            pl.BlockSpec((SEQ_PAD, d), lambda l, j: (0, 0)),
            pl.BlockSpec((None, d, d), lambda l, j: (l, 0, j)),
            pl.BlockSpec((None, None, 1, d), lambda l, j: (l, j, 0, 0)),
        ],
        out_specs=pl.BlockSpec((None, None, SEQ_PAD, d), lambda l, j: (l, j, 0, 0)),
        out_shape=jax.ShapeDtypeStruct((depth, n_mod, SEQ_PAD, d), F32),
        compiler_params=pltpu.CompilerParams(
            dimension_semantics=("parallel", "arbitrary"),
            vmem_limit_bytes=_vmem_limit(blocks, temp_bytes=_nbytes((d, d), BF16))),
        name="ada_table",
    )(c_all, ada_w, ada_b.reshape(depth, n_mod, 1, d))


def _ffn_kernel(h0_ref, xc_ref, xa_ref, mod_ref, ng_ref, w1_ref, w3_ref, w2_ref, o_ref, ha_ref, hb_ref, *,
                tm, sub_cols, ahead, lay):
    i = pl.program_id(0)
    j = pl.program_id(1)

    @pl.when(jnp.logical_and(i == 0, j == 0))
    def _():
        pltpu.sync_copy(h0_ref, ha_ref)

    @pl.when(j == 0)
    def _():
        o_ref[...] = jnp.zeros_like(o_ref)

    def step(h_ref, h_next):
        row0 = pl.multiple_of(jnp.minimum(j, ahead.blocks - 1) * ahead.rows, CHUNK)
        o_ref[pl.ds(row0, ahead.rows), :] += jnp.where(j < ahead.blocks, xc_ref[...], 0.0)
        h = h_ref[...]
        pieces = []
        for c0 in range(0, w1_ref.shape[-1], sub_cols):
            cols = slice(c0, c0 + sub_cols)
            a = _dot(h, w1_ref[:, cols].astype(BF16))
            b = _dot(h, w3_ref[:, cols].astype(BF16))
            pieces.append((a * jax.nn.sigmoid(a) * b).astype(BF16))
        g = pieces[0] if len(pieces) == 1 else jnp.concatenate(pieces, axis=1)
        p = _dot(g, w2_ref[...].astype(BF16))
        for grp in range(tm // CHUNK):
            rows = slice(grp * CHUNK, (grp + 1) * CHUNK)
            gate = MACARON * _gate_row(mod_ref, i * tm + grp * CHUNK, lay)
            o_ref[rows, :] += gate * p[rows, :]
        ahead.run(h_next, xa_ref, mod_ref, ng_ref, i, j, lay)

    _by_parity(i, ha_ref, hb_ref, step)


def _ffn(x, mod, norm_g4, w1, w3, w2, layer, sub, which, lay):
    t, d = x.shape
    f = w1.shape[-1]
    tm = _pick(t, (1024, 768, 512, 384, 256, 128))
    tf = _pick(f, (512, 256, 128))
    nj = f // tf
    ahead = AheadPlan(tm, t // tm, nj)
    h0 = _first_tile_h(x, mod, norm_g4, layer, sub, tm, lay)
    blocks = (_nbytes((tm, d), F32) + 2 * _nbytes((ahead.rows, d), F32) + _nbytes(mod.shape[2:], F32)
              + 3 * _nbytes((d, tf), w1.dtype))
    temps = 2 * _nbytes((tm, tf), F32) + _nbytes((tm, tf), BF16) + 3 * _nbytes((d, tf), BF16)
    return pl.pallas_call(
        functools.partial(_ffn_kernel, tm=tm, sub_cols=_pick(tf, (256, 128)), ahead=ahead, lay=lay),
        grid=(t // tm, nj),
        in_specs=[
            pl.BlockSpec(memory_space=pl.ANY),
            ahead.current_spec(d),
            ahead.spec(d),
            _mod_spec(mod, layer, sub, 2),
            _norm_g_spec(norm_g4, layer, sub),
            pl.BlockSpec((None, None, d, tf), lambda i, j: (layer, which, 0, j)),
            pl.BlockSpec((None, None, d, tf), lambda i, j: (layer, which, 0, j)),
            pl.BlockSpec((None, None, tf, d), lambda i, j: (layer, which, j, 0)),
        ],
        out_specs=pl.BlockSpec((tm, d), lambda i, j: (i, 0)),
        out_shape=jax.ShapeDtypeStruct((t, d), F32),
        scratch_shapes=[pltpu.VMEM((tm, d), BF16), pltpu.VMEM((tm, d), BF16)],
        compiler_params=pltpu.CompilerParams(
            dimension_semantics=("arbitrary", "arbitrary"),
            vmem_limit_bytes=_vmem_limit(blocks, _nbytes((2, tm, d), BF16), temps)),
        name="ffn",
    )(h0, x, x, mod, norm_g4, w1, w3, w2)


def _inproj_kernel(h0_ref, xa_ref, mod_ref, ng_ref, *refs, mode, has_bias, tm, sub_cols, ahead, lay):
    i = pl.program_id(0)
    j = pl.program_id(1)
    ha_ref, hb_ref = refs[-2:]

    @pl.when(jnp.logical_and(i == 0, j == 0))
    def _():
        pltpu.sync_copy(h0_ref, ha_ref)

    def step(h_ref, h_next):
        h = h_ref[...]
        if mode == "single":
            if has_bias:
                w_ref, b_ref, o_ref = refs[:3]
                o_ref[...] = _dot(h, w_ref[...].astype(BF16)) + b_ref[...]
            else:
                w_ref, o_ref = refs[:2]
                o_ref[...] = _dot(h, w_ref[...].astype(BF16))
        elif mode == "glu":
            wa_ref, wb_ref, ba_ref, bb_ref, o_ref = refs[:5]
            for c0 in range(0, o_ref.shape[-1], sub_cols):
                cols = slice(c0, c0 + sub_cols)
                a = _dot(h, wa_ref[:, cols].astype(BF16)) + ba_ref[:, cols]
                b = _dot(h, wb_ref[:, cols].astype(BF16)) + bb_ref[:, cols]
                o_ref[:, cols] = a * jax.nn.sigmoid(b)
        else:
            wa_ref, wb_ref, ba_ref, bb_ref, ou_ref, ov_ref = refs[:6]
            for c0 in range(0, ou_ref.shape[-1], sub_cols):
                cols = slice(c0, c0 + sub_cols)
                ou_ref[:, cols] = jax.nn.gelu(_dot(h, wa_ref[:, cols].astype(BF16)) + ba_ref[:, cols],
                                              approximate=True).astype(ou_ref.dtype)
                ov_ref[:, cols] = jax.nn.gelu(_dot(h, wb_ref[:, cols].astype(BF16)) + bb_ref[:, cols],
                                              approximate=True)
        ahead.run(h_next, xa_ref, mod_ref, ng_ref, i, j, lay)

    _by_parity(i, ha_ref, hb_ref, step)


def _inproj(x, mod, norm_g4, layer, w, b, lay, mode):
    t, d = x.shape
    n = w.shape[1]
    tm = _pick(t, (1536, 1024, 768, 512, 384, 256, 128))
    tn = _pick(n if mode == "single" else n // 2, (512, 256, 128))
    nj = (n if mode == "single" else n // 2) // tn
    ahead = AheadPlan(tm, t // tm, nj)
    h0 = _first_tile_h(x, mod, norm_g4, layer, 1, tm, lay)
    x_spec = pl.BlockSpec(memory_space=pl.ANY)
    mod_spec = _mod_spec(mod, layer, 1, 2)
    ng_spec = _norm_g_spec(norm_g4, layer, 1)
    lo = pl.BlockSpec((tm, tn), lambda i, j: (i, j))
    if mode == "single":
        ins = [h0, x, mod, norm_g4, w]
        specs = [x_spec, ahead.spec(d), mod_spec, ng_spec, pl.BlockSpec((d, tn), lambda i, j: (0, j))]
        if b is not None:
            ins.append(b.reshape(1, n))
            specs.append(pl.BlockSpec((1, tn), lambda i, j: (0, j)))
        out_shape = jax.ShapeDtypeStruct((t, n), F32)
        out_specs = lo
        n_w = 1
    else:
        half = n // 2
        ins = [h0, x, mod, norm_g4, w, w, b.reshape(1, n), b.reshape(1, n)]
        specs = [x_spec, ahead.spec(d), mod_spec, ng_spec,
                 pl.BlockSpec((d, tn), lambda i, j: (0, j)),
                 pl.BlockSpec((d, tn), lambda i, j: (0, j + nj)),
                 pl.BlockSpec((1, tn), lambda i, j: (0, j)),
                 pl.BlockSpec((1, tn), lambda i, j: (0, j + nj))]
        n_w = 2
        if mode == "glu":
            out_shape = jax.ShapeDtypeStruct((t, half), F32)
            out_specs = lo
        else:
            out_shape = (jax.ShapeDtypeStruct((t, half), BF16), jax.ShapeDtypeStruct((t, half), F32))
            out_specs = (lo, lo)
    blocks = (_nbytes((ahead.rows, d), F32) + _nbytes(mod.shape[2:], F32)
              + n_w * _nbytes((d, tn), w.dtype) + n_w * _nbytes((tm, tn), F32))
    return pl.pallas_call(
        functools.partial(_inproj_kernel, mode=mode, has_bias=b is not None, tm=tm,
                          sub_cols=_pick(tn, (256, 128)), ahead=ahead, lay=lay),
        grid=(t // tm, nj),
        in_specs=specs,
        out_specs=out_specs,
        out_shape=out_shape,
        scratch_shapes=[pltpu.VMEM((tm, d), BF16), pltpu.VMEM((tm, d), BF16)],
        compiler_params=pltpu.CompilerParams(
            dimension_semantics=("arbitrary", "arbitrary"),
            vmem_limit_bytes=_vmem_limit(blocks, _nbytes((2, tm, d), BF16),
                                         n_w * (3 * _nbytes((tm, tn), F32) + _nbytes((d, tn), BF16)))),
        name="inproj_" + mode,
    )(*ins)


def _outproj_kernel(x_ref, a_ref, mod_ref, w_ref, *refs, has_bias, tm, nk, lay):
    i = pl.program_id(0)
    k = pl.program_id(1)
    b_ref, o_ref = (refs[0], refs[1]) if has_bias else (None, refs[0])
    groups = tm // CHUNK

    def gates():
        for grp in range(groups):
            yield slice(grp * CHUNK, (grp + 1) * CHUNK), _gate_row(mod_ref, i * tm + grp * CHUNK, lay)

    p = _dot(a_ref[...], w_ref[...])
    if nk == 1:
        for rows, gate in gates():
            acc = p[rows, :] if b_ref is None else p[rows, :] + b_ref[...]
            o_ref[rows, :] = x_ref[rows, :] + gate * acc
    else:
        @pl.when(k == 0)
        def _():
            for rows, gate in gates():
                o_ref[rows, :] = x_ref[rows, :] if b_ref is None else x_ref[rows, :] + gate * b_ref[...]

        for rows, gate in gates():
            o_ref[rows, :] += gate * p[rows, :]


def _outproj(x, a, mod, layer, w, b, lay):
    t, d = x.shape
    kdim = a.shape[1]
    tm = _pick(t, (512, 384, 256, 128))
    tk = _pick(kdim, (2048, 1024, 512))
    nk = kdim // tk
    ins = [x, a, mod, w]
    specs = [pl.BlockSpec((tm, d), lambda i, k: (i, 0)),
             pl.BlockSpec((tm, tk), lambda i, k: (i, k)),
             _mod_spec(mod, layer, 1, 2),
             pl.BlockSpec((tk, d), lambda i, k: (k, 0))]
    if b is not None:
        ins.append(b.reshape(1, d))
        specs.append(pl.BlockSpec((1, d), lambda i, k: (0, 0)))
    blocks = (2 * _nbytes((tm, d), F32) + _nbytes((tm, tk), BF16) + _nbytes(mod.shape[2:], F32)
              + _nbytes((tk, d), BF16))
    return pl.pallas_call(
        functools.partial(_outproj_kernel, has_bias=b is not None, tm=tm, nk=nk, lay=lay),
        grid=(t // tm, nk),
        in_specs=specs,
        out_specs=pl.BlockSpec((tm, d), lambda i, k: (i, 0)),
        out_shape=jax.ShapeDtypeStruct((t, d), F32),
        compiler_params=pltpu.CompilerParams(
            dimension_semantics=("parallel", "arbitrary"),
            vmem_limit_bytes=_vmem_limit(blocks)),
        name="outproj",
    )(*ins)


def _fill_history(ext_ref, prev_ref, cache_ref, n, n_prompt_tiles, tiles_per_seq):
    rows = prev_ref.shape[0]
    is_sample = n >= n_prompt_tiles
    starts_seq = jnp.logical_and(n % tiles_per_seq == 0, jnp.logical_not(is_sample))

    @pl.when(is_sample)
    def _():
        ext_ref[0:rows, :] = cache_ref[...]

    @pl.when(starts_seq)
    def _():
        ext_ref[0:rows, :] = jnp.zeros(prev_ref.shape, ext_ref.dtype)

    @pl.when(jnp.logical_not(jnp.logical_or(is_sample, starts_seq)))
    def _():
        ext_ref[0:rows, :] = prev_ref[...]


def _history_specs(rows, d, n_prompt_tiles, sub=1, s=0):
    per_tile = CHUNK // rows
    return [pl.BlockSpec((None, rows, d), lambda i: (jnp.maximum((i * sub + s) * per_tile - 1, 0), 0, 0)),
            pl.BlockSpec((None, rows, d), lambda i: (jnp.maximum(i * sub + s - n_prompt_tiles, 0), 0, 0))]


def _pad_cache(cache, rows):
    return jnp.pad(cache, ((0, 0), (rows - cache.shape[1], 0), (0, 0)))


def _conv_mid_kernel(g_ref, *refs, sub, cb, n_prompt_tiles, tiles_per_seq):
    hist_refs = refs[:2 * sub]
    wdw_ref, bdw_ref, lng_ref, lnb_ref, o_ref, ext_ref, y_ref = refs[2 * sub:]
    for s in range(sub):
        _conv_tile(g_ref, hist_refs[2 * s], hist_refs[2 * s + 1], wdw_ref, bdw_ref, lng_ref, lnb_ref, o_ref,
                   ext_ref, y_ref, pl.program_id(0) * sub + s, slice(s * CHUNK, (s + 1) * CHUNK), cb,
                   n_prompt_tiles, tiles_per_seq)


def _conv_tile(g_ref, prev_ref, cache_ref, wdw_ref, bdw_ref, lng_ref, lnb_ref, o_ref, ext_ref, y_ref,
               tile, rows, cb, n_prompt_tiles, tiles_per_seq):
    d = g_ref.shape[-1]
    tt = CHUNK
    _fill_history(ext_ref, prev_ref, cache_ref, tile, n_prompt_tiles, tiles_per_seq)
    ext_ref[CONV_HALO:CONV_HALO + tt, :] = g_ref[rows, :]
    ext_ref[CONV_HALO + tt:CONV_HALO + tt + SUBLANES, :] = jnp.zeros((SUBLANES, d), F32)
    first = CONV_HALO - CONV_STATE
    for c0 in range(0, d, cb):
        cols = slice(c0, c0 + cb)
        y = None
        for r in range(SUBLANES):
            u = None
            for k in range(CONV_WIDTH):
                if (first + k) % SUBLANES != r:
                    continue
                a0 = (first + k) - r
                term = ext_ref[a0:a0 + tt + SUBLANES, cols] * wdw_ref[k:k + 1, cols]
                u = term if u is None else u + term
            if u is None:
                continue
            piece = u[r:r + tt, :]
            y = piece if y is None else y + piece
        y_ref[:, cols] = y + bdw_ref[:, cols]
    y = y_ref[...]
    yc = y - jnp.mean(y, axis=-1, keepdims=True)
    var = jnp.mean(yc * yc, axis=-1, keepdims=True)
    z = yc * lax.rsqrt(var + EPS) * lng_ref[...] + lnb_ref[...]
    o_ref[rows, :] = (z * jax.nn.sigmoid(z)).astype(o_ref.dtype)


def _conv_mid(g, cache, w_dw, b_dw, ln_g, ln_b, lay):
    t, d = g.shape
    sub = 2 if (t // CHUNK) % 2 == 0 else 1
    tt = sub * CHUNK
    n_prompt_tiles = lay.tp // CHUNK
    cb = 2 * LANES
    hist = []
    hist_in = []
    for s in range(sub):
        hist += _history_specs(CONV_HALO, d, n_prompt_tiles, sub, s)
        hist_in += [g.reshape(t // CONV_HALO, CONV_HALO, d), _pad_cache(cache, CONV_HALO)]
    blocks = (_nbytes((tt, d), F32) + 2 * sub * _nbytes((CONV_HALO, d), F32) + _nbytes((32, d), F32)
              + 3 * _nbytes((SUBLANES, d), F32) + _nbytes((tt, d), BF16))
    vec = pl.BlockSpec((1, d), lambda i: (0, 0))
    ext_rows = CONV_HALO + CHUNK + SUBLANES
    return pl.pallas_call(
        functools.partial(_conv_mid_kernel, sub=sub, cb=cb, n_prompt_tiles=n_prompt_tiles,
                          tiles_per_seq=lay.sp // CHUNK),
        grid=(t // tt,),
        in_specs=[pl.BlockSpec((tt, d), lambda i: (i, 0))] + hist + [
            pl.BlockSpec((CONV_WIDTH, d), lambda i: (0, 0)),
            vec, vec, vec,
        ],
        out_specs=pl.BlockSpec((tt, d), lambda i: (i, 0)),
        out_shape=jax.ShapeDtypeStruct((t, d), BF16),
        scratch_shapes=[pltpu.VMEM((ext_rows, d), F32), pltpu.VMEM((CHUNK, d), F32)],
        compiler_params=pltpu.CompilerParams(
            dimension_semantics=("parallel",),
            vmem_limit_bytes=_vmem_limit(blocks, _nbytes((ext_rows + CHUNK, d), F32), 4 * _nbytes((CHUNK, d), F32))),
        name="conv_mid",
    )(g, *hist_in, w_dw, b_dw.reshape(1, d), ln_g.reshape(1, d), ln_b.reshape(1, d))


def _pool_mid_kernel(p_ref, *refs, sub, n_prompt_tiles, tiles_per_seq):
    hist_refs = refs[:2 * sub]
    wg_ref, sc_ref, o_ref, ext_ref, pooled_ref = refs[2 * sub:]
    n = pl.program_id(0)
    d = p_ref.shape[-1]
    gw = d // len(POOL_WINDOWS)
    for s in range(sub):
        tile = n * sub + s
        rows = slice(s * CHUNK, (s + 1) * CHUNK)
        _fill_history(ext_ref, hist_refs[2 * s], hist_refs[2 * s + 1], tile, n_prompt_tiles, tiles_per_seq)
        ext_ref[POOL_HALO:POOL_HALO + CHUNK, :] = p_ref[rows, :]
        pos0 = jnp.where(tile < n_prompt_tiles, (tile % tiles_per_seq) * CHUNK, PAST_LEN)
        pos = pos0 + lax.broadcasted_iota(jnp.int32, (CHUNK, 1), 0)
        for gi, w in enumerate(POOL_WINDOWS):
            cols = slice(gi * gw, (gi + 1) * gw)
            acc = ext_ref[POOL_HALO:POOL_HALO + CHUNK, cols]
            for back in range(1, w):
                acc = acc + ext_ref[POOL_HALO - back:POOL_HALO - back + CHUNK, cols]
            inv_cnt = 1.0 / jnp.minimum(pos + 1, w).astype(F32)
            pooled_ref[rows, cols] = (acc * inv_cnt - p_ref[rows, cols]).astype(pooled_ref.dtype)
    for gi in range(len(POOL_WINDOWS)):
        cols = slice(gi * gw, (gi + 1) * gw)
        z = _dot(pooled_ref[:, cols], wg_ref[gi]) * sc_ref[:, cols]
        o_ref[:, cols] = z.astype(o_ref.dtype)


def _pool_mid(p, cache, w_grp, scale, lay):
    t, d = p.shape
    sub = 4
    tt = sub * CHUNK
    assert lay.tp % tt == 0 and lay.tsamp % tt == 0
    n_prompt_tiles = lay.tp // CHUNK
    hist = []
    hist_in = []
    for s in range(sub):
        hist += _history_specs(POOL_HALO, d, n_prompt_tiles, sub, s)
        hist_in += [p.reshape(t // POOL_HALO, POOL_HALO, d), _pad_cache(cache, POOL_HALO)]
    blocks = (_nbytes((tt, d), F32) + 2 * sub * _nbytes((POOL_HALO, d), F32) + _nbytes(w_grp.shape, BF16)
              + _nbytes((SUBLANES, d), F32) + _nbytes((tt, d), BF16))
    return pl.pallas_call(
        functools.partial(_pool_mid_kernel, sub=sub, n_prompt_tiles=n_prompt_tiles,
                          tiles_per_seq=lay.sp // CHUNK),
        grid=(t // tt,),
        in_specs=[pl.BlockSpec((tt, d), lambda i: (i, 0))] + hist + [
            pl.BlockSpec(w_grp.shape, lambda i: (0, 0, 0)),
            pl.BlockSpec((1, d), lambda i: (0, 0)),
        ],
        out_specs=pl.BlockSpec((tt, d), lambda i: (i, 0)),
        out_shape=jax.ShapeDtypeStruct((t, d), BF16),
        scratch_shapes=[pltpu.VMEM((POOL_HALO + CHUNK, d), F32), pltpu.VMEM((tt, d), BF16)],
        compiler_params=pltpu.CompilerParams(
            dimension_semantics=("parallel",),
            vmem_limit_bytes=_vmem_limit(blocks, _nbytes((POOL_HALO + CHUNK, d), F32) + _nbytes((tt, d), BF16),
                                         4 * _nbytes((tt, d), F32))),
        name="pool_mid",
    )(p, *hist_in, w_grp, scale.reshape(1, d))


def _attend_chunk(q_ref, k_refs, v_refs, bias_ref, sink_ref, o_ref, rows):
    k0_ref, k1_ref, k2_ref = k_refs
    v0_ref, v1_ref, v2_ref = v_refs
    keys = SWA_BAND + CHUNK
    pair_heads = Q_PER_KV // 2
    low_half = lax.broadcasted_iota(jnp.int32, (keys, LANES), 1) < HEAD_DIM
    tn_dims = (((1,), (1,)), ((), ()))
    nt_dims = (((0,), (0,)), ((), ()))

    units = []
    for m in range(N_KV_HEADS // 2):
        cols = slice(m * LANES, (m + 1) * LANES)
        kb = jnp.concatenate([k0_ref[:, cols], k1_ref[:, cols], k2_ref[:, cols]], axis=0) * ATTN_SCALE
        vb = jnp.concatenate([v0_ref[:, cols], v1_ref[:, cols], v2_ref[:, cols]], axis=0)
        kswap = pltpu.roll(kb, HEAD_DIM, axis=1)
        vswap = pltpu.roll(vb, HEAD_DIM, axis=1)
        for sub in range(2):
            kh = 2 * m + sub
            k_low, k_high = (kb, kswap) if sub == 0 else (kswap, kb)
            v_low, v_high = (vb, vswap) if sub == 0 else (vswap, vb)
            units.append((kh, 0, jnp.where(low_half, k_low, 0.0).astype(BF16),
                          jnp.where(low_half, v_low, 0.0).astype(BF16)))
            units.append((kh, 1, jnp.where(low_half, 0.0, k_high).astype(BF16),
                          jnp.where(low_half, 0.0, v_high).astype(BF16)))

    qstacks = []
    for kh in range(N_KV_HEADS):
        qstacks.append(jnp.concatenate(
            [q_ref[rows, (kh * pair_heads + j) * LANES:(kh * pair_heads + j + 1) * LANES] for j in range(pair_heads)],
            axis=0).astype(BF16))

    scores = []
    for kh, par, kpad, _ in units:
        s = lax.dot_general(kpad, qstacks[kh], tn_dims, preferred_element_type=F32)
        scores.append(s + bias_ref[2 * kh + par])

    probs = []
    for (kh, par, _, _), s in zip(units, scores):
        sk = sink_ref[2 * kh + par]
        mx = jnp.maximum(jnp.max(s, axis=0, keepdims=True), sk)
        e = jnp.exp(s - mx)
        denom = jnp.sum(e, axis=0, keepdims=True) + jnp.exp(sk - mx)
        probs.append((e * (1.0 / denom)).astype(BF16))

    for kh in range(N_KV_HEADS):
        acc = None
        for (ukh, par, _, vpad), p in zip(units, probs):
            if ukh != kh:
                continue
            o = lax.dot_general(p, vpad, nt_dims, preferred_element_type=F32)
            acc = o if acc is None else acc + o
        for j in range(pair_heads):
            c0 = (kh * pair_heads + j) * LANES
            o_ref[rows, c0:c0 + LANES] = acc[j * CHUNK:(j + 1) * CHUNK, :].astype(o_ref.dtype)


def _attn_kernel(q_ref, *refs, cpb):
    n_kv = SWA_BAND // CHUNK + 1
    k_refs = refs[:cpb * n_kv]
    v_refs = refs[cpb * n_kv:2 * cpb * n_kv]
    bias_refs = refs[2 * cpb * n_kv:2 * cpb * n_kv + cpb]
    sink_ref, o_ref = refs[2 * cpb * n_kv + cpb:]
    for ci in range(cpb):
        _attend_chunk(q_ref, k_refs[ci * n_kv:(ci + 1) * n_kv], v_refs[ci * n_kv:(ci + 1) * n_kv],
                      bias_refs[ci], sink_ref, o_ref, slice(ci * CHUNK, (ci + 1) * CHUNK))


def _attention(qkv, kpad, vpad, bias, sinks, lay):
    t = qkv.shape[0]
    hw = N_HEADS * HEAD_DIM
    kvw = N_KV_HEADS * HEAD_DIM
    n_hist = SWA_BAND // CHUNK
    cps = lay.sp // CHUNK
    n_prompt_chunks = lay.bp * cps

    cpb = 2 if (t // CHUNK) % 2 == 0 else 1

    def kv_map(ci, off):
        def index(step):
            c = step * cpb + ci
            base = jnp.where(c < n_prompt_chunks,
                             (c // cps) * (cps + n_hist) + c % cps,
                             lay.bp * (cps + n_hist) + (c - n_prompt_chunks) * (n_hist + 1))
            return (base + off, 0)
        return index

    def bias_map(ci):
        def index(step):
            c = step * cpb + ci
            return (jnp.where(c < n_prompt_chunks, jnp.minimum(c % cps, n_hist), n_hist), 0, 0, 0)
        return index

    kv_specs = [pl.BlockSpec((CHUNK, kvw), kv_map(ci, o)) for ci in range(cpb) for o in range(n_hist + 1)]
    bias_specs = [pl.BlockSpec((None,) + bias.shape[1:], bias_map(ci)) for ci in range(cpb)]
    rows = cpb * CHUNK
    blocks = (_nbytes((rows, hw), F32) + 2 * len(kv_specs) * _nbytes((CHUNK, kvw), F32)
              + cpb * _nbytes(bias.shape[1:], F32)
              + _nbytes((2 * N_KV_HEADS, SUBLANES, bias.shape[-1]), F32) + _nbytes((rows, hw), BF16))
    return pl.pallas_call(
        functools.partial(_attn_kernel, cpb=cpb),
        grid=(t // rows,),
        in_specs=[pl.BlockSpec((rows, hw), lambda c: (c, 0))] + kv_specs + kv_specs + bias_specs + [
            pl.BlockSpec(sinks.shape, lambda c: (0, 0, 0)),
        ],
        out_specs=pl.BlockSpec((rows, hw), lambda c: (c, 0)),
        out_shape=jax.ShapeDtypeStruct((t, hw), BF16),
        compiler_params=pltpu.CompilerParams(
            dimension_semantics=("parallel",),
            vmem_limit_bytes=_vmem_limit(blocks, 0, 6 * cpb * _nbytes(bias.shape[1:], F32))),
        name="swa_attention",
    )(qkv, *([kpad] * len(kv_specs)), *([vpad] * len(kv_specs)), *([bias] * cpb), sinks)


def _alibi_tables(swa_sinks):
    n_hist = SWA_BAND // CHUNK
    keys = SWA_BAND + CHUNK
    pair_heads = Q_PER_KV // 2

    def by_unit(per_head):
        return per_head.reshape(N_KV_HEADS, pair_heads, 2).transpose(0, 2, 1).reshape(2 * N_KV_HEADS, pair_heads)

    slopes = by_unit(jnp.exp2(-8.0 * jnp.arange(1, N_HEADS + 1, dtype=F32) / N_HEADS))
    dist = jnp.abs(jnp.arange(CHUNK)[None, :] + SWA_BAND - jnp.arange(keys)[:, None]).astype(F32)
    bias = (-slopes[:, None, :, None] * dist[None, :, None, :]).reshape(2 * N_KV_HEADS, keys, pair_heads * CHUNK)
    key_idx = jnp.arange(keys)[None, :, None]
    bias = jnp.stack([jnp.where(key_idx >= (n_hist - v) * CHUNK, bias, -jnp.inf) for v in range(n_hist + 1)])
    sinks = jnp.broadcast_to(by_unit(swa_sinks.astype(F32))[:, None, :, None],
                             (2 * N_KV_HEADS, 1, pair_heads, CHUNK)).reshape(2 * N_KV_HEADS, 1, pair_heads * CHUNK)
    return bias, sinks


def _gmlp_mid_kernel(u_ref, v_ref, lng_ref, lnb_ref, ws_ref, bs_ref, o_ref, vn_ref, *, n_prompt_blocks):
    n = pl.program_id(0)
    is_sample = n >= n_prompt_blocks
    v = v_ref[...]
    vc = v - jnp.mean(v, axis=-1, keepdims=True)
    var = jnp.mean(vc * vc, axis=-1, keepdims=True)
    vn = vc * lax.rsqrt(var + EPS) * lng_ref[...] + lnb_ref[...]
    vn_ref[...] = vn
    vnb = vn.astype(BF16)
    ll = GMLP_CHUNK
    rb = lax.broadcasted_iota(jnp.int32, (ll, ll), 0) // CHUNK
    cb = lax.broadcasted_iota(jnp.int32, (ll, ll), 1) // CHUNK
    back = rb - cb
    mask = (back >= 0) & (back <= jnp.where(is_sample, 0, 1))
    gw = u_ref.shape[-1] // GMLP_GROUPS
    for g in range(GMLP_GROUPS):
        w = jnp.where(mask, ws_ref[g], 0.0).astype(BF16)
        mixed = _dot(w, vnb[:, g * gw:(g + 1) * gw]) + bs_ref[g]
        o_ref[:, g * gw:(g + 1) * gw] = (u_ref[:, g * gw:(g + 1) * gw].astype(F32) * mixed).astype(o_ref.dtype)


def _gmlp_mid(u, v, ln_g, ln_b, ws2, bs2, n_prompt_blocks):
    t, gwid = u.shape
    ll = GMLP_CHUNK
    sel = lambda n: jnp.where(n >= n_prompt_blocks, 1, 0)
    blocks = (_nbytes((ll, gwid), BF16) + 2 * _nbytes((ll, gwid), F32) + 2 * _nbytes((SUBLANES, gwid), F32)
              + _nbytes((GMLP_GROUPS, ll, ll), F32) + _nbytes((GMLP_GROUPS, ll, LANES), F32)
              + _nbytes((ll, gwid), BF16))
    return pl.pallas_call(
        functools.partial(_gmlp_mid_kernel, n_prompt_blocks=n_prompt_blocks),
        grid=(t // ll,),
        in_specs=[
            pl.BlockSpec((ll, gwid), lambda n: (n, 0)),
            pl.BlockSpec((ll, gwid), lambda n: (n, 0)),
            pl.BlockSpec((1, gwid), lambda n: (0, 0)),
            pl.BlockSpec((1, gwid), lambda n: (0, 0)),
            pl.BlockSpec((None, GMLP_GROUPS, ll, ll), lambda n: (sel(n), 0, 0, 0)),
            pl.BlockSpec((None, GMLP_GROUPS, ll, 1), lambda n: (sel(n), 0, 0, 0)),
        ],
        out_specs=(pl.BlockSpec((ll, gwid), lambda n: (n, 0)),
                   pl.BlockSpec((ll, gwid), lambda n: (jnp.maximum(n - n_prompt_blocks, 0), 0))),
        out_shape=(jax.ShapeDtypeStruct((t, gwid), BF16),
                   jax.ShapeDtypeStruct((t - n_prompt_blocks * ll, gwid), F32)),
        compiler_params=pltpu.CompilerParams(
            dimension_semantics=("arbitrary",),
            vmem_limit_bytes=_vmem_limit(blocks, 0, 6 * _nbytes((ll, gwid), F32))),
        name="gmlp_mid",
    )(u, v, ln_g.reshape(1, gwid), ln_b.reshape(1, gwid), ws2, bs2)


def _final_norm_kernel(x_ref, g_ref, op_ref, os_ref, *, n_prompt_tiles):
    x = x_ref[...]
    ms = jnp.mean(x * x, axis=-1, keepdims=True)
    y = x * lax.rsqrt(ms + EPS) * g_ref[...]
    os_ref[...] = y

    @pl.when(pl.program_id(0) < n_prompt_tiles)
    def _():
        op_ref[...] = y


def _final_norm(x, g, lay):
    t, d = x.shape
    tm = _pick(lay.tp, (256, 128))
    assert lay.tsamp % tm == 0
    npt = lay.tp // tm
    return pl.pallas_call(
        functools.partial(_final_norm_kernel, n_prompt_tiles=npt),
        grid=(t // tm,),
        in_specs=[pl.BlockSpec((tm, d), lambda i: (i, 0)), pl.BlockSpec((1, d), lambda i: (0, 0))],
        out_specs=(pl.BlockSpec((tm, d), lambda i: (jnp.minimum(i, npt - 1), 0)),
                   pl.BlockSpec((tm, d), lambda i: (jnp.maximum(i - npt, 0), 0))),
        out_shape=(jax.ShapeDtypeStruct((lay.tp, d), F32), jax.ShapeDtypeStruct((lay.tsamp, d), F32)),
        compiler_params=pltpu.CompilerParams(
            dimension_semantics=("arbitrary",),
            vmem_limit_bytes=_vmem_limit(3 * _nbytes((tm, d), F32), 0, 2 * _nbytes((tm, d), F32))),
        name="final_norm",
    )(x, g.reshape(1, d))


def _state_tail(y, cache, keep, lay):
    d = y.shape[-1]
    tails = jnp.stack([y[(b + 1) * lay.sp - keep:(b + 1) * lay.sp] for b in range(lay.bp)])
    ys = y[lay.tp:].reshape(lay.bs, lay.ts, d)
    return tails, jnp.concatenate([cache, ys], axis=1)[:, -keep:]


def kernel(x_prompt, x_sample, c_prompt, c_sample, cache_conv, cache_pool, cache_swa_k, cache_swa_v, ada_w, ada_b, norm_g, ffn_w1, ffn_w3, ffn_w2, conv_w_pw1, conv_b_pw1, conv_w_dw, conv_b_dw, conv_ln_g, conv_ln_b, conv_w_pw2, conv_b_pw2, pool_w_in, pool_w_grp, pool_scale, pool_w_out, swa_wq, swa_wk, swa_wv, swa_wo, swa_sinks, gmlp_w_in, gmlp_b_in, gmlp_ln_g, gmlp_ln_b, gmlp_w_s, gmlp_b_s, gmlp_w_out, gmlp_b_out, final_g):
    bp, sp, d = x_prompt.shape
    bs, ts, _ = x_sample.shape
    lay = Layout(bp, sp, bs, ts)
    depth = ada_w.shape[0]
    kvw = N_KV_HEADS * HEAD_DIM
    hw = N_HEADS * HEAD_DIM

    x = jnp.concatenate([x_prompt.reshape(lay.tp, d), x_sample.reshape(lay.tsamp, d)], axis=0)
    c_all = jnp.pad(jnp.concatenate([c_prompt, c_sample], axis=0), ((0, SEQ_PAD - bp - bs), (0, 0)))
    mod = _ada_table(c_all, ada_w, ada_b).reshape(depth, 3, 3, SEQ_PAD, d)
    norm_g4 = norm_g.reshape(depth, 3, 1, d)

    outs = {}
    for i in range(depth):
        x = _ffn(x, mod, norm_g4, ffn_w1, ffn_w3, ffn_w2, i, 0, 0, lay)
        kind = i % 4
        if kind == 0:
            g = _inproj(x, mod, norm_g4, i, conv_w_pw1, conv_b_pw1, lay, "glu")
            a = _conv_mid(g, cache_conv, conv_w_dw, conv_b_dw, conv_ln_g, conv_ln_b, lay)
            x = _outproj(x, a.reshape(lay.t, d), mod, i, conv_w_pw2.astype(BF16), conv_b_pw2, lay)
            outs["conv"] = _state_tail(g, cache_conv, CONV_STATE, lay)
        elif kind == 1:
            p = _inproj(x, mod, norm_g4, i, pool_w_in, None, lay, "single")
            a = _pool_mid(p, cache_pool, pool_w_grp.astype(BF16), pool_scale, lay)
            x = _outproj(x, a.reshape(lay.t, d), mod, i, pool_w_out.astype(BF16), None, lay)
            outs["pool"] = _state_tail(p, cache_pool, POOL_STATE, lay)
        elif kind == 2:
            w_qkv = jnp.concatenate([swa_wq, swa_wk, swa_wv], axis=1)
            qkv = _inproj(x, mod, norm_g4, i, w_qkv, None, lay, "single")
            r = cache_swa_k.shape[1]
            assert r == SWA_BAND

            def padded(col0, cache):
                new = qkv[:, col0:col0 + kvw]
                newp = jnp.pad(new[:lay.tp].reshape(bp, sp, kvw), ((0, 0), (SWA_BAND, 0), (0, 0)))
                news = jnp.concatenate([cache.reshape(bs, r, kvw), new[lay.tp:].reshape(bs, ts, kvw)], axis=1)
                tail_p = new[:lay.tp].reshape(bp, sp, kvw)[:, sp - SWA_BAND:]
                tail_s = news[:, ts:]
                both = jnp.concatenate([newp.reshape(-1, kvw), news.reshape(-1, kvw)], axis=0)
                return both, tail_p.reshape(bp, SWA_BAND, N_KV_HEADS, HEAD_DIM), tail_s.reshape(bs, r, N_KV_HEADS, HEAD_DIM)

            kpad, k_p, k_s = padded(hw, cache_swa_k)
            vpad, v_p, v_s = padded(hw + kvw, cache_swa_v)
            bias, sinks = _alibi_tables(swa_sinks)
            a = _attention(qkv, kpad, vpad, bias, sinks, lay)
            x = _outproj(x, a, mod, i, swa_wo.astype(BF16), None, lay)
            outs["kv"] = (k_p, v_p, k_s, v_s)
        else:
            u, v = _inproj(x, mod, norm_g4, i, gmlp_w_in, gmlp_b_in, lay, "gelu2")
            ll = GMLP_CHUNK
            ws2 = jnp.stack([gmlp_w_s[:, :ll, :ll], jnp.tile(gmlp_w_s[:, :ts, :ts], (1, ll // ts, ll // ts))])
            bs2 = jnp.stack([gmlp_b_s[:, :ll], jnp.tile(gmlp_b_s[:, :ts], (1, ll // ts))])[..., None]
            a, vn = _gmlp_mid(u, v, gmlp_ln_g, gmlp_ln_b, ws2, bs2, lay.tp // ll)
            x = _outproj(x, a, mod, i, gmlp_w_out.astype(BF16), gmlp_b_out, lay)
            outs["gmlp_v"] = vn.reshape(bs, ts, -1)
        x = _ffn(x, mod, norm_g4, ffn_w1, ffn_w3, ffn_w2, i, 2, 1, lay)

    y_p, y_s = _final_norm(x, final_g, lay)
    y_prompt = y_p.reshape(bp, sp, d)
    y_sample = y_s.reshape(bs, ts, d)
    conv_p, conv_s = outs["conv"]
    pool_p, pool_s = outs["pool"]
    k_p, v_p, k_s, v_s = outs["kv"]
    return (y_prompt, y_sample, conv_p, conv_s, pool_p, pool_s, k_p, v_p, k_s, v_s, outs["gmlp_v"])
```
